```python
import math
import jax, jax.numpy as jnp
from jax import lax
import numpy as np

D_MODEL = 2048
BATCH = 4
SEQ = 4096
DEPTH = 4

HEAD_DIM = 128
GROUP_WIDTH = D_MODEL // 4
N_HEADS = GROUP_WIDTH // HEAD_DIM
D_MIX = 4 * GROUP_WIDTH
DILATED_PAIRS = ((128, 1), (512, 4), (2048, 16))
ROPE_THETA = 10000.0
CONV_WIDTH = 4
LRU_BLOCKS = N_HEADS
LRU_C = 8.0
GDN_CHUNK = 64
MLSTM_CHUNK = 64
N_EXPERTS = 16
N_EXPERT_GROUPS = 4
EXPERTS_PER_GROUP = N_EXPERTS // N_EXPERT_GROUPS
TOP_K = 2
D_FF = D_MODEL // 4
EPS = 1e-6
M_INIT = -1e30

A_COLS = 3 * GROUP_WIDTH
B_COLS = 2 * GROUP_WIDTH
C_COLS = 4 * GROUP_WIDTH + 2 * N_HEADS
D_COLS = 4 * GROUP_WIDTH + 2 * N_HEADS
IN_COLS = A_COLS + B_COLS + C_COLS + D_COLS

kernel_name = "hybrid_parallel_mixer_moe_trunk"


def rms_norm(x, gain):
    xf = x.astype(jnp.float32)
    y = xf * lax.rsqrt(jnp.mean(xf * xf, axis=-1, keepdims=True) + EPS)
    return (y * gain.astype(jnp.float32)).astype(x.dtype)


def l2_norm(x):
    xf = x.astype(jnp.float32)
    return xf * lax.rsqrt(jnp.sum(xf * xf, axis=-1, keepdims=True) + EPS)


def heads(t):
    b, s, _ = t.shape
    return t.reshape(b, s, N_HEADS, HEAD_DIM)


def rotary(x, positions):
    half = x.shape[-1] // 2
    inv_freq = ROPE_THETA ** (-jnp.arange(half, dtype=jnp.float32) / half)
    ang = positions.astype(jnp.float32)[:, None] * inv_freq[None, :]
    cos = jnp.cos(ang)[None, :, None, :]
    sin = jnp.sin(ang)[None, :, None, :]
    xf = x.astype(jnp.float32)
    x1, x2 = xf[..., :half], xf[..., half:]
    return jnp.concatenate([x1 * cos - x2 * sin, x2 * cos + x1 * sin], axis=-1).astype(x.dtype)


def causal_depthwise_conv(x, w):
    k, ch = w.shape
    return lax.conv_general_dilated(
        x, w[:, None, :].astype(x.dtype), window_strides=(1,), padding=((k - 1, 0),),
        dimension_numbers=("NWC", "WIO", "NWC"), feature_group_count=ch)


def dilated_branch(q, k, v, window, dilation):
    b, s, h, dh = q.shape
    blk = window // dilation
    length = s // dilation
    nb = -(-length // blk)
    lp = nb * blk

    def to_strided(t):
        t = t.reshape(b, length, dilation, h, dh).transpose(0, 2, 3, 1, 4)
        t = jnp.pad(t, ((0, 0), (0, 0), (0, 0), (0, lp - length), (0, 0)))
        return t.reshape(b, dilation, h, nb, blk, dh)

    def with_prev(t):
        prev = jnp.pad(t[:, :, :, :-1], ((0, 0), (0, 0), (0, 0), (1, 0), (0, 0), (0, 0)))
        return jnp.concatenate([prev, t], axis=4)

    qb = to_strided(q)
    kw = with_prev(to_strided(k))
    vw = with_prev(to_strided(v))
    sc = jnp.einsum("brhnqd,brhnkd->brhnqk", qb, kw,
                    preferred_element_type=jnp.float32) / math.sqrt(dh)
    rel = (jnp.arange(blk)[:, None] + blk) - jnp.arange(2 * blk)[None, :]
    band = (rel >= 0) & (rel <= blk)
    own = jnp.arange(2 * blk) >= blk
    valid = band[None] & ((jnp.arange(nb) > 0)[:, None, None] | own[None, None, :])
    sc = jnp.where(valid, sc, -jnp.inf)
    m = jnp.max(sc, axis=-1)
    p = jnp.exp(sc - m[..., None])
    den = jnp.sum(p, axis=-1)
    num = jnp.einsum("brhnqk,brhnkd->brhnqd", p, vw.astype(jnp.float32))

    def from_strided(t):
        rest = t.shape[5:]
        t = t.reshape(b, dilation, h, lp, *rest)[:, :, :, :length]
        t = jnp.moveaxis(t, 3, 1)
        return t.reshape(b, s, h, *rest)

    return from_strided(num), from_strided(m), from_strided(den)


def dilated_attention(q, k, v):
    branches = [dilated_branch(q, k, v, w, d) for (w, d) in DILATED_PAIRS]
    m_all = branches[0][1]
    for br in branches[1:]:
        m_all = jnp.maximum(m_all, br[1])
    num = jnp.zeros(q.shape, jnp.float32)
    den = jnp.zeros(q.shape[:-1], jnp.float32)
    for br_num, br_m, br_den in branches:
        wgt = jnp.exp(br_m - m_all)
        num = num + br_num * wgt[..., None]
        den = den + br_den * wgt
    return (num / den[..., None]).astype(q.dtype)


def rg_lru(x, w_r, b_r, w_i, b_i, lam):
    b, s, w = x.shape
    xb = x.reshape(b, s, LRU_BLOCKS, w // LRU_BLOCKS)
    r = jax.nn.sigmoid(jnp.einsum("bsgi,gio->bsgo", xb, w_r).reshape(b, s, w) + b_r)
    i = jax.nn.sigmoid(jnp.einsum("bsgi,gio->bsgo", xb, w_i).reshape(b, s, w) + b_i)
    log_a = -LRU_C * r.astype(jnp.float32) * jax.nn.softplus(-lam.astype(jnp.float32))
    a = jnp.exp(log_a)
    u = jnp.sqrt(-jnp.expm1(2.0 * log_a)) * (i * x).astype(jnp.float32)

    def combine(e1, e2):
        a1, b1 = e1
        a2, b2 = e2
        return a1 * a2, a2 * b1 + b2

    _, hs = lax.associative_scan(combine, (a, u), axis=1)
    return hs.astype(x.dtype)


def gated_delta_net(q, k, v, g, beta):
    b, s, h, dk = q.shape
    dv = v.shape[-1]
    c = GDN_CHUNK
    n = s // c

    def chunks(t):
        return jnp.moveaxis(t.reshape(b, n, c, h, *t.shape[3:]), 3, 1)

    q, k, v, g, beta = [chunks(t) for t in (q, k, v, g, beta)]
    gc = jnp.cumsum(g, axis=-1)
    incl = jnp.tril(jnp.ones((c, c), bool))
    strict = jnp.tril(jnp.ones((c, c), bool), -1)
    decay = jnp.exp(jnp.where(incl, gc[..., :, None] - gc[..., None, :], -jnp.inf))
    kb = k * beta[..., None]
    vb = v * beta[..., None]
    a_mat = jnp.where(strict, jnp.einsum("bhnid,bhnjd->bhnij", kb, k) * decay, 0.0)
    t_mat = a_mat + jnp.eye(c, dtype=jnp.float32)
    u = lax.linalg.triangular_solve(t_mat, vb, left_side=True, lower=True, unit_diagonal=True)
    w = lax.linalg.triangular_solve(t_mat, kb * jnp.exp(gc)[..., None],
                                    left_side=True, lower=True, unit_diagonal=True)
    qk = jnp.einsum("bhnid,bhnjd->bhnij", q, k) * decay
    q_dec = q * jnp.exp(gc)[..., None]
    k_dec = k * jnp.exp(gc[..., -1:] - gc)[..., None]
    g_last = jnp.exp(gc[..., -1])

    def step(state, xs):
        u_i, w_i, qk_i, qd_i, kd_i, gl_i = xs
        v_new = u_i - jnp.einsum("bhcd,bhde->bhce", w_i, state)
        o_i = (jnp.einsum("bhcd,bhde->bhce", qd_i, state)
               + jnp.einsum("bhij,bhje->bhie", qk_i, v_new))
        state = state * gl_i[..., None, None] + jnp.einsum("bhcd,bhce->bhde", kd_i, v_new)
        return state, o_i

    xs = [jnp.moveaxis(t, 2, 0) for t in (u, w, qk, q_dec, k_dec, g_last)]
    _, o = lax.scan(step, jnp.zeros((b, h, dk, dv), jnp.float32), xs)
    return o.transpose(1, 0, 3, 2, 4).reshape(b, s, h, dv)


def mlstm(q, k, v, i_pre, f_pre):
    b, s, h, dk = q.shape
    dv = v.shape[-1]
    c = MLSTM_CHUNK
    n = s // c

    def chunks(t):
        t = t.reshape(b, n, c, h, *t.shape[3:])
        return jnp.moveaxis(jnp.moveaxis(t, 1, 0), 3, 2)

    log_f = jax.nn.log_sigmoid(f_pre)
    qc, kc, vc, ic, fc = [chunks(t) for t in (q, k, v, i_pre, log_f)]
    bc = jnp.cumsum(fc, axis=-1)
    incl = jnp.tril(jnp.ones((c, c), bool))

    def step(carry, xs):
        c_st, n_st, m_st = carry
        q_i, k_i, v_i, ig_i, b_i = xs
        log_d = jnp.where(incl, b_i[..., :, None] - b_i[..., None, :] + ig_i[..., None, :], -jnp.inf)
        inter = b_i + m_st[..., None]
        m_t = jnp.maximum(inter, jnp.max(log_d, axis=-1))
        d_mat = jnp.exp(log_d - m_t[..., None])
        e_inter = jnp.exp(inter - m_t)
        sc = jnp.einsum("bhtd,bhsd->bhts", q_i, k_i) * d_mat
        num = (e_inter[..., None] * jnp.einsum("bhtd,bhde->bhte", q_i, c_st)
               + jnp.einsum("bhts,bhse->bhte", sc, v_i))
        den = e_inter * jnp.einsum("bhtd,bhd->bht", q_i, n_st) + jnp.sum(sc, axis=-1)
        h_i = num / jnp.maximum(jnp.abs(den), jnp.exp(-m_t))[..., None]
        b_last = b_i[..., -1]
        log_w = b_last[..., None] - b_i + ig_i
        m_new = jnp.maximum(b_last + m_st, jnp.max(log_w, axis=-1))
        wk = jnp.exp(log_w - m_new[..., None])
        keep = jnp.exp(b_last + m_st - m_new)
        c_st = keep[..., None, None] * c_st + jnp.einsum("bhs,bhsd,bhse->bhde", wk, k_i, v_i)
        n_st = keep[..., None] * n_st + jnp.einsum("bhs,bhsd->bhd", wk, k_i)
        return (c_st, n_st, m_new), h_i

    carry0 = (jnp.zeros((b, h, dk, dv), jnp.float32), jnp.zeros((b, h, dk), jnp.float32),
              jnp.full((b, h), M_INIT, jnp.float32))
    _, hs = lax.scan(step, carry0, (qc, kc, vc, ic, bc))
    return hs.transpose(1, 0, 3, 2, 4).reshape(b, s, h, dv)


def hybrid_mixer(h, w_in, w_out, gn_a, conv_b_w, conv_b_b, w_rg, b_rg, w_ig, b_ig, lru_lambda,
                 gn_b, conv_c_w, gdn_a_log, gdn_dt_bias, gdn_norm, mlstm_i_bias, mlstm_f_bias, gn_d):
    b, s, _ = h.shape
    dt = h.dtype
    proj = jnp.einsum("bsd,dc->bsc", h, w_in)
    a_part, b_part, c_part, d_part = jnp.split(
        proj, [A_COLS, A_COLS + B_COLS, A_COLS + B_COLS + C_COLS], axis=-1)

    pos = jnp.arange(s)
    qa, ka, va = [heads(t) for t in jnp.split(a_part, 3, axis=-1)]
    qa, ka = rotary(qa, pos), rotary(ka, pos)
    y_a = rms_norm(dilated_attention(qa, ka, va), gn_a.reshape(N_HEADS, HEAD_DIM))

    xb, gb = jnp.split(b_part, 2, axis=-1)
    xb = causal_depthwise_conv(xb, conv_b_w) + conv_b_b
    hb = heads(rg_lru(xb, w_rg, b_rg, w_ig, b_ig, lru_lambda))
    y_b = rms_norm(hb, gn_b.reshape(N_HEADS, HEAD_DIM)) * heads(jax.nn.gelu(gb))

    qkv_c, z_c, beta_pre, alpha_pre = jnp.split(
        c_part, [3 * GROUP_WIDTH, 4 * GROUP_WIDTH, 4 * GROUP_WIDTH + N_HEADS], axis=-1)
    qkv_c = jax.nn.silu(causal_depthwise_conv(qkv_c, conv_c_w))
    qc, kc, vc = [heads(t) for t in jnp.split(qkv_c, 3, axis=-1)]
    qc = l2_norm(qc) * (HEAD_DIM ** -0.5)
    kc = l2_norm(kc)
    beta = jax.nn.sigmoid(beta_pre.astype(jnp.float32))
    g = -jnp.exp(gdn_a_log.astype(jnp.float32)) * jax.nn.softplus(
        alpha_pre.astype(jnp.float32) + gdn_dt_bias.astype(jnp.float32))
    oc = gated_delta_net(qc, kc, vc.astype(jnp.float32), g, beta).astype(dt)
    y_c = rms_norm(oc, gdn_norm) * heads(jax.nn.silu(z_c))

    qd, kd, vd, od, i_pre, f_pre = jnp.split(
        d_part, [GROUP_WIDTH, 2 * GROUP_WIDTH, 3 * GROUP_WIDTH, 4 * GROUP_WIDTH,
                 4 * GROUP_WIDTH + N_HEADS], axis=-1)
    hd = mlstm(heads(qd).astype(jnp.float32),
               heads(kd).astype(jnp.float32) / math.sqrt(HEAD_DIM),
               heads(vd).astype(jnp.float32),
               (i_pre + mlstm_i_bias).astype(jnp.float32),
               (f_pre + mlstm_f_bias).astype(jnp.float32)).astype(dt)
    y_d = rms_norm(hd, gn_d.reshape(N_HEADS, HEAD_DIM)) * heads(jax.nn.sigmoid(od))

    y = jnp.concatenate([t.reshape(b, s, GROUP_WIDTH).astype(dt) for t in (y_a, y_b, y_c, y_d)],
                        axis=-1)
    return jnp.einsum("bsc,cd->bsd", y, w_out)


def grouped_moe(h, w_router, router_bias, w_gate, w_up, w_down):
    b, s, _ = h.shape
    scores = jax.nn.sigmoid(jnp.einsum("bsd,de->bse", h, w_router).astype(jnp.float32))
    biased = scores + router_bias.astype(jnp.float32)
    grouped = biased.reshape(b, s, N_EXPERT_GROUPS, EXPERTS_PER_GROUP)
    group_score = jnp.sum(lax.top_k(grouped, TOP_K)[0], axis=-1)
    sel_group = jnp.argmax(group_score, axis=-1)
    in_group = (jnp.arange(N_EXPERTS) // EXPERTS_PER_GROUP)[None, None, :] == sel_group[..., None]
    _, idx = lax.top_k(jnp.where(in_group, biased, -jnp.inf), TOP_K)
    w_sel = jnp.take_along_axis(scores, idx, axis=-1)
    w_sel = w_sel / jnp.sum(w_sel, axis=-1, keepdims=True)
    gates = jnp.sum(jax.nn.one_hot(idx, N_EXPERTS, dtype=jnp.float32) * w_sel[..., None], axis=-2)
    hg = jnp.einsum("bsd,edf->bsef", h, w_gate)
    hu = jnp.einsum("bsd,edf->bsef", h, w_up)
    act = jax.nn.silu(hg) * hu * gates[..., None].astype(h.dtype)
    return jnp.einsum("bsef,efd->bsd", act, w_down)


def setup_inputs(seed: int = 0) -> dict:
    key = jax.random.key(seed)
    keys = iter(jax.random.split(key, 40))
    nrm = lambda shape, scale: jax.random.normal(next(keys), shape, jnp.float32) * scale
    gain = lambda shape: 1.0 + nrm(shape, 0.05)
    x = nrm((BATCH, SEQ, D_MODEL), 1.0)
    c = nrm((BATCH, D_MODEL), 1.0)
    norm_mix = gain((DEPTH, D_MODEL))
    norm_ffn = gain((DEPTH, D_MODEL))
    norm_final = gain((D_MODEL,))
    w_ada = nrm((DEPTH, D_MODEL, 6 * D_MODEL), 0.5 * D_MODEL ** -0.5)
    b_ada = nrm((DEPTH, 6 * D_MODEL), 0.02)
    w_in = nrm((DEPTH, D_MODEL, IN_COLS), D_MODEL ** -0.5)
    w_out = nrm((DEPTH, D_MIX, D_MODEL), D_MIX ** -0.5)
    gn_a = gain((DEPTH, GROUP_WIDTH))
    conv_b_w = nrm((DEPTH, CONV_WIDTH, GROUP_WIDTH), CONV_WIDTH ** -0.5)
    conv_b_b = nrm((DEPTH, GROUP_WIDTH), 0.02)
    bw = GROUP_WIDTH // LRU_BLOCKS
    w_rg = nrm((DEPTH, LRU_BLOCKS, bw, bw), bw ** -0.5)
    b_rg = nrm((DEPTH, GROUP_WIDTH), 0.02)
    w_ig = nrm((DEPTH, LRU_BLOCKS, bw, bw), bw ** -0.5)
    b_ig = nrm((DEPTH, GROUP_WIDTH), 0.02)
    a8 = jax.random.uniform(next(keys), (DEPTH, GROUP_WIDTH), jnp.float32, 0.9, 0.999)
    a_base = a8 ** (1.0 / LRU_C)
    lru_lambda = jnp.log(a_base) - jnp.log1p(-a_base)
    gn_b = gain((DEPTH, GROUP_WIDTH))
    conv_c_w = nrm((DEPTH, CONV_WIDTH, 3 * GROUP_WIDTH), CONV_WIDTH ** -0.5)
    gdn_a_log = jnp.log(jax.random.uniform(next(keys), (DEPTH, N_HEADS), jnp.float32, 1.0, 16.0))
    dt0 = jnp.exp(jax.random.uniform(next(keys), (DEPTH, N_HEADS), jnp.float32,
                                     math.log(1e-3), math.log(1e-1)))
    gdn_dt_bias = dt0 + jnp.log(-jnp.expm1(-dt0))
    gdn_norm = gain((DEPTH, HEAD_DIM))
    mlstm_i_bias = nrm((DEPTH, N_HEADS), 0.1)
    mlstm_f_bias = jax.random.uniform(next(keys), (DEPTH, N_HEADS), jnp.float32, 3.0, 6.0)
    gn_d = gain((DEPTH, GROUP_WIDTH))
    w_router = nrm((D_MODEL, N_EXPERTS), D_MODEL ** -0.5)
    router_bias = nrm((N_EXPERTS,), 0.01)
    moe_w_gate = nrm((DEPTH, N_EXPERTS, D_MODEL, D_FF), D_MODEL ** -0.5)
    moe_w_up = nrm((DEPTH, N_EXPERTS, D_MODEL, D_FF), D_MODEL ** -0.5)
    moe_w_down = nrm((DEPTH, N_EXPERTS, D_FF, D_MODEL), D_FF ** -0.5)
    return {"x": x, "c": c, "norm_mix": norm_mix, "norm_ffn": norm_ffn, "norm_final": norm_final,
            "w_ada": w_ada, "b_ada": b_ada, "w_in": w_in, "w_out": w_out, "gn_a": gn_a,
            "conv_b_w": conv_b_w, "conv_b_b": conv_b_b, "w_rg": w_rg, "b_rg": b_rg,
            "w_ig": w_ig, "b_ig": b_ig, "lru_lambda": lru_lambda, "gn_b": gn_b,
            "conv_c_w": conv_c_w, "gdn_a_log": gdn_a_log, "gdn_dt_bias": gdn_dt_bias,
            "gdn_norm": gdn_norm, "mlstm_i_bias": mlstm_i_bias, "mlstm_f_bias": mlstm_f_bias,
            "gn_d": gn_d, "w_router": w_router, "router_bias": router_bias,
            "moe_w_gate": moe_w_gate, "moe_w_up": moe_w_up, "moe_w_down": moe_w_down}


def reference(x, c, norm_mix, norm_ffn, norm_final, w_ada, b_ada, w_in, w_out, gn_a,
              conv_b_w, conv_b_b, w_rg, b_rg, w_ig, b_ig, lru_lambda, gn_b,
              conv_c_w, gdn_a_log, gdn_dt_bias, gdn_norm, mlstm_i_bias, mlstm_f_bias,
              gn_d, w_router, router_bias, moe_w_gate, moe_w_up, moe_w_down):
    c_act = jax.nn.silu(c)
    for l in range(DEPTH):
        mod = jnp.einsum("bd,de->be", c_act, w_ada[l]) + b_ada[l]
        sh1, sc1, g1, sh2, sc2, g2 = jnp.split(mod[:, None, :], 6, axis=-1)
        h = rms_norm(x, norm_mix[l]) * (1.0 + sc1) + sh1
        x = x + g1 * hybrid_mixer(h, w_in[l], w_out[l], gn_a[l], conv_b_w[l], conv_b_b[l],
                                  w_rg[l], b_rg[l], w_ig[l], b_ig[l], lru_lambda[l], gn_b[l],
                                  conv_c_w[l], gdn_a_log[l], gdn_dt_bias[l], gdn_norm[l],
                                  mlstm_i_bias[l], mlstm_f_bias[l], gn_d[l])
        h = rms_norm(x, norm_ffn[l]) * (1.0 + sc2) + sh2
        x = x + g2 * grouped_moe(h, w_router, router_bias, moe_w_gate[l], moe_w_up[l], moe_w_down[l])
    return rms_norm(x, norm_final)
```

```python
import functools
import math

import jax
import jax.numpy as jnp
from jax import lax
from jax.experimental import pallas as pl
from jax.experimental.pallas import tpu as pltpu

F32 = jnp.float32
BF16 = jnp.bfloat16
HIGHEST = lax.Precision.HIGHEST

HEAD_DIM = 128
N_HEADS = 4
GROUP_WIDTH = N_HEADS * HEAD_DIM
DILATED_PAIRS = ((128, 1), (512, 4), (2048, 16))
ATTN_BLOCK = 128
ROPE_THETA = 10000.0
CONV_WIDTH = 4
LRU_C = 8.0
CHUNK = 64
N_EXPERTS = 16
EXPERTS_PER_GROUP = 4
EPS = 1e-6
M_INIT = -1e30
NEG = -1e30
SUBLANES = 8
VMEM_LIMIT = 56 * 1024 * 1024

COL_AQ, COL_AK, COL_AV = 0, 4, 8
COL_BX, COL_BG = 3, 4
COL_CQ, COL_CK, COL_CV, COL_CZ = 5, 6, 7, 8
COL_DQ, COL_DK, COL_DV, COL_DO = 9, 10, 11, 12
MAIN_COLS = 13 * GROUP_WIDTH
GATE_LANES = 128


def _cparams(sem):
    return pltpu.CompilerParams(dimension_semantics=sem, vmem_limit_bytes=VMEM_LIMIT)


def _dot(a, b):
    return jnp.dot(a.astype(BF16), b.astype(BF16), preferred_element_type=F32)


def _dot_nt(a, b):
    return lax.dot_general(a.astype(BF16), b.astype(BF16), (((1,), (1,)), ((), ())),
                           preferred_element_type=F32)


def _dot_tn(a, b):
    return lax.dot_general(a.astype(BF16), b.astype(BF16), (((0,), (0,)), ((), ())),
                           preferred_element_type=F32)


def _dot_exact(a, b):
    return jnp.dot(a, b, precision=HIGHEST, preferred_element_type=F32)


def _row_form(col_b):
    c = col_b.shape[0]
    lane = lax.broadcasted_iota(jnp.int32, col_b.shape, 1)
    picked = jnp.where(lane == 0, col_b, 0.0)
    return lax.dot_general(jnp.ones((c, col_b.shape[1]), F32), picked, (((1,), (1,)), ((), ())),
                           precision=HIGHEST, preferred_element_type=F32)


def _softplus(z):
    return jnp.maximum(z, 0.0) + jnp.log1p(jnp.exp(-jnp.abs(z)))


def _sigmoid(z):
    return 1.0 / (1.0 + jnp.exp(-z))


def _silu(z):
    return z * _sigmoid(z)


def _head_rms(x, gain):
    outs = []
    for g in range(x.shape[1] // HEAD_DIM):
        xg = x[:, g * HEAD_DIM:(g + 1) * HEAD_DIM]
        ms = jnp.mean(xg * xg, axis=-1, keepdims=True)
        outs.append(xg * lax.rsqrt(ms + EPS))
    y = outs[0] if len(outs) == 1 else jnp.concatenate(outs, axis=1)
    return y * gain


def _tril(c, strict=False):
    row = lax.broadcasted_iota(jnp.int32, (c, c), 0)
    col = lax.broadcasted_iota(jnp.int32, (c, c), 1)
    return (col < row) if strict else (col <= row)


def _ada_kernel(c_ref, w_ref, b_ref, o_ref):
    c = c_ref[...]
    o_ref[0] = _dot(_silu(c), w_ref[0]) + b_ref[0]


def ada_modulation(c, w_ada, b_ada):
    depth, d, n = w_ada.shape
    b = c.shape[0]
    rows = -(-b // SUBLANES) * SUBLANES
    c_pad = jnp.zeros((rows, d), F32).at[:b].set(c)
    tn = 1024
    out = pl.pallas_call(
        _ada_kernel,
        grid=(depth, n // tn),
        in_specs=[pl.BlockSpec((rows, d), lambda l, j: (0, 0)),
                  pl.BlockSpec((1, d, tn), lambda l, j: (l, 0, j)),
                  pl.BlockSpec((1, 1, tn), lambda l, j: (l, 0, j))],
        out_specs=pl.BlockSpec((1, rows, tn), lambda l, j: (l, 0, j)),
        out_shape=jax.ShapeDtypeStruct((depth, rows, n), F32),
        compiler_params=_cparams(("arbitrary", "arbitrary")),
        name="ada_modulation",
    )(c_pad, w_ada, b_ada.reshape(depth, 1, n))
    return out[:, :b]


def _modulated_norm(x, nw, sc, sh):
    ms = jnp.mean(x * x, axis=-1, keepdims=True)
    return (x * lax.rsqrt(ms + EPS) * nw) * (1.0 + sc) + sh


def _in_proj_kernel(x_ref, sc_ref, sh_ref, nw_ref, w_ref, wg_ref, proj_ref, gates_ref, h_scr):
    @pl.when(pl.program_id(2) == 0)
    def _():
        h = _modulated_norm(x_ref[0], nw_ref[...], sc_ref[0], sh_ref[0]).astype(BF16)
        h_scr[...] = h
        gates_ref[0] = jnp.dot(h, wg_ref[...], preferred_element_type=F32)

    proj_ref[0] = jnp.dot(h_scr[...], w_ref[...], preferred_element_type=F32).astype(BF16)


def in_projection(x, scale, shift, norm_w, w_main, w_gate, tm=512, tn=1664):
    b, s, d = x.shape
    n = w_main.shape[1]
    return pl.pallas_call(
        _in_proj_kernel,
        grid=(b, s // tm, n // tn),
        in_specs=[pl.BlockSpec((1, tm, d), lambda bi, i, j: (bi, i, 0)),
                  pl.BlockSpec((1, 1, d), lambda bi, i, j: (bi, 0, 0)),
                  pl.BlockSpec((1, 1, d), lambda bi, i, j: (bi, 0, 0)),
                  pl.BlockSpec((1, d), lambda bi, i, j: (0, 0)),
                  pl.BlockSpec((d, tn), lambda bi, i, j: (0, j)),
                  pl.BlockSpec((d, GATE_LANES), lambda bi, i, j: (0, 0))],
        out_specs=[pl.BlockSpec((1, tm, tn), lambda bi, i, j: (bi, i, j)),
                   pl.BlockSpec((1, tm, GATE_LANES), lambda bi, i, j: (bi, i, 0))],
        out_shape=[jax.ShapeDtypeStruct((b, s, n), BF16),
                   jax.ShapeDtypeStruct((b, s, GATE_LANES), F32)],
        scratch_shapes=[pltpu.VMEM((tm, d), BF16)],
        compiler_params=_cparams(("arbitrary", "arbitrary", "arbitrary")),
        name="in_projection",
    )(x, scale, shift, norm_w, w_main, w_gate)


def _out_proj_kernel(x_ref, ya_ref, yb_ref, yc_ref, yd_ref, w_ref, g_ref, o_ref):
    acc = None
    for k, y_ref in enumerate((ya_ref, yb_ref, yc_ref, yd_ref)):
        part = jnp.dot(y_ref[0], w_ref[k * GROUP_WIDTH:(k + 1) * GROUP_WIDTH, :],
                       preferred_element_type=F32)
        acc = part if acc is None else acc + part
    o_ref[0] = x_ref[0] + g_ref[0] * acc


def out_projection(x, ys, w_out, gate, tm=512):
    b, s, d = x.shape
    yspec = pl.BlockSpec((1, tm, GROUP_WIDTH), lambda bi, i: (bi, i, 0))
    return pl.pallas_call(
        _out_proj_kernel,
        grid=(b, s // tm),
        in_specs=[pl.BlockSpec((1, tm, d), lambda bi, i: (bi, i, 0)),
                  yspec, yspec, yspec, yspec,
                  pl.BlockSpec(w_out.shape, lambda bi, i: (0, 0)),
                  pl.BlockSpec((1, 1, d), lambda bi, i: (bi, 0, 0))],
        out_specs=pl.BlockSpec((1, tm, d), lambda bi, i: (bi, i, 0)),
        out_shape=jax.ShapeDtypeStruct((b, s, d), F32),
        compiler_params=_cparams(("arbitrary", "arbitrary")),
        name="out_projection",
    )(x, *ys, w_out, gate)


def _attn_kernel(q_ref, k_ref, v_ref, cos_ref, sin_ref, gn_ref, o_ref, qs, ks, vs, acc, ms, ls):
    s = q_ref.shape[1]
    blk = ATTN_BLOCK
    cos = cos_ref[...]
    sin = sin_ref[...]

    def rot(t):
        return t * cos + pltpu.roll(t, HEAD_DIM // 2, 1) * sin

    qs[...] = rot(q_ref[0].astype(F32)) * (1.0 / math.sqrt(HEAD_DIM))
    ks[...] = rot(k_ref[0].astype(F32))
    vs[...] = v_ref[0].astype(F32)

    row = lax.broadcasted_iota(jnp.int32, (blk, 2 * blk), 0)
    col = lax.broadcasted_iota(jnp.int32, (blk, 2 * blk), 1)
    own_ok = (col >= blk) & (col - blk <= row)

    for branch, (window, dil) in enumerate(DILATED_PAIRS):
        nb = s // (blk * dil)

        def rows(start, dil=dil):
            if dil == 1:
                return pl.ds(pl.multiple_of(start, blk), blk)
            return pl.ds(start, blk, stride=dil)

        def body(idx, carry, dil=dil, nb=nb, branch=branch, rows=rows):
            r = idx // nb
            n = idx % nb
            start = r + n * (blk * dil)
            pstart = jnp.where(n > 0, start - blk * dil, start)
            qb = qs[rows(start), :]
            kc = jnp.concatenate([ks[rows(pstart), :], ks[rows(start), :]], axis=0)
            vc = jnp.concatenate([vs[rows(pstart), :], vs[rows(start), :]], axis=0)
            sc = _dot_nt(qb, kc)
            first = jnp.where(n > 0, 0, 4 * blk)
            valid = own_ok | ((col < blk) & (col >= row + first))
            sc = jnp.where(valid, sc, NEG)
            mb = jnp.max(sc, axis=-1, keepdims=True)
            p = jnp.exp(sc - mb)
            den = jnp.sum(p, axis=-1, keepdims=True)
            num = _dot(p, vc)
            mb = jnp.broadcast_to(mb, (blk, HEAD_DIM))
            den = jnp.broadcast_to(den, (blk, HEAD_DIM))
            if branch == 0:
                acc[rows(start), :] = num
                ms[rows(start), :] = mb
                ls[rows(start), :] = den
            else:
                m_old = ms[rows(start), :]
                m_new = jnp.maximum(m_old, mb)
                a_old = jnp.exp(m_old - m_new)
                a_blk = jnp.exp(mb - m_new)
                acc[rows(start), :] = acc[rows(start), :] * a_old + num * a_blk
                ls[rows(start), :] = ls[rows(start), :] * a_old + den * a_blk
                ms[rows(start), :] = m_new
            return carry

        lax.fori_loop(0, s // blk, body, 0)

    out = acc[...] / ls[...]
    o_ref[0] = _head_rms(out, gn_ref[...]).astype(o_ref.dtype)


def dilated_attention_mixer(proj, cos2, sin2, gn_a):
    b, s, _ = proj.shape
    hd = HEAD_DIM

    def col(off):
        return pl.BlockSpec((1, s, hd), lambda bi, h, off=off: (bi, 0, off + h))

    return pl.pallas_call(
        _attn_kernel,
        grid=(b, N_HEADS),
        in_specs=[col(COL_AQ), col(COL_AK), col(COL_AV),
                  pl.BlockSpec((s, hd), lambda bi, h: (0, 0)),
                  pl.BlockSpec((s, hd), lambda bi, h: (0, 0)),
                  pl.BlockSpec((1, hd), lambda bi, h: (0, h))],
        out_specs=pl.BlockSpec((1, s, hd), lambda bi, h: (bi, 0, h)),
        out_shape=jax.ShapeDtypeStruct((b, s, GROUP_WIDTH), BF16),
        scratch_shapes=[pltpu.VMEM((s, hd), F32) for _ in range(6)],
        compiler_params=_cparams(("arbitrary", "arbitrary")),
        name="dilated_attention",
    )(proj, proj, proj, cos2, sin2, gn_a)


def _load_conv_tile(x_ref, xpad, ts):
    @pl.when(pl.program_id(1) == 0)
    def _():
        xpad[0:SUBLANES, :] = jnp.zeros((SUBLANES, xpad.shape[1]), F32)

    @pl.when(pl.program_id(1) > 0)
    def _():
        xpad[0:SUBLANES, :] = xpad[ts:ts + SUBLANES, :]

    xpad[SUBLANES:SUBLANES + ts, :] = x_ref[0].astype(F32)


def _conv_from_pad(xpad, w, ts):
    y = None
    for j in range(CONV_WIDTH):
        off = SUBLANES - (CONV_WIDTH - 1) + j
        term = xpad[off:off + ts, :] * w[j:j + 1, :]
        y = term if y is None else y + term
    return y


def _lru_kernel(x_ref, g_ref, cw_ref, cb_ref, wr_ref, br_ref, wi_ref, bi_ref, lam_ref, gn_ref,
                o_ref, xpad, a_scr, u_scr, carry):
    ts = x_ref.shape[1]
    w = x_ref.shape[2]
    _load_conv_tile(x_ref, xpad, ts)

    @pl.when(pl.program_id(1) == 0)
    def _():
        carry[...] = jnp.zeros(carry.shape, F32)

    xc = _conv_from_pad(xpad, cw_ref[...], ts) + cb_ref[...]
    r_parts, i_parts = [], []
    for g in range(N_HEADS):
        xg = xc[:, g * HEAD_DIM:(g + 1) * HEAD_DIM]
        r_parts.append(_dot(xg, wr_ref[g]))
        i_parts.append(_dot(xg, wi_ref[g]))
    r = _sigmoid(jnp.concatenate(r_parts, axis=1) + br_ref[...])
    ig = _sigmoid(jnp.concatenate(i_parts, axis=1) + bi_ref[...])
    log_a = (-LRU_C) * r * _softplus(-lam_ref[...])
    a_scr[...] = jnp.exp(log_a)
    u_scr[...] = jnp.sqrt(1.0 - jnp.exp(2.0 * log_a)) * (ig * xc)

    sub = lax.broadcasted_iota(jnp.int32, (SUBLANES, w), 0)

    def body(j, h_prev):
        rows = pl.ds(pl.multiple_of(j * SUBLANES, SUBLANES), SUBLANES)
        a = a_scr[rows, :]
        u = u_scr[rows, :]
        for sh in (1, 2, 4):
            keep = sub >= sh
            a_sh = jnp.where(keep, pltpu.roll(a, sh, 0), 1.0)
            u_sh = jnp.where(keep, pltpu.roll(u, sh, 0), 0.0)
            u = a * u_sh + u
            a = a * a_sh
        h = u + a * h_prev
        u_scr[rows, :] = h
        return jnp.broadcast_to(h[SUBLANES - 1:SUBLANES, :], (SUBLANES, w))

    carry[...] = lax.fori_loop(0, ts // SUBLANES, body, carry[...])
    y = _head_rms(u_scr[...], gn_ref[...]) * jax.nn.gelu(g_ref[0].astype(F32))
    o_ref[0] = y.astype(o_ref.dtype)


def rg_lru_mixer(proj, conv_w, conv_b, w_rg, b_rg, w_ig, b_ig, lam, gn_b, ts=1024):
    b, s, _ = proj.shape
    w = GROUP_WIDTH
    vec = pl.BlockSpec((1, w), lambda bi, i: (0, 0))
    blockdiag = pl.BlockSpec((N_HEADS, HEAD_DIM, HEAD_DIM), lambda bi, i: (0, 0, 0))
    return pl.pallas_call(
        _lru_kernel,
        grid=(b, s // ts),
        in_specs=[pl.BlockSpec((1, ts, w), lambda bi, i: (bi, i, COL_BX)),
                  pl.BlockSpec((1, ts, w), lambda bi, i: (bi, i, COL_BG)),
                  pl.BlockSpec((CONV_WIDTH, w), lambda bi, i: (0, 0)),
                  vec, blockdiag, vec, blockdiag, vec, vec, vec],
        out_specs=pl.BlockSpec((1, ts, w), lambda bi, i: (bi, i, 0)),
        out_shape=jax.ShapeDtypeStruct((b, s, w), BF16),
        scratch_shapes=[pltpu.VMEM((ts + SUBLANES, w), F32), pltpu.VMEM((ts, w), F32),
                        pltpu.VMEM((ts, w), F32), pltpu.VMEM((SUBLANES, w), F32)],
        compiler_params=_cparams(("arbitrary", "arbitrary")),
        name="rg_lru",
    )(proj, proj, conv_w, conv_b, w_rg, b_rg, w_ig, b_ig, lam, gn_b)


def _lane_pick(x, lane_idx):
    lane = lax.broadcasted_iota(jnp.int32, x.shape, 1)
    col = jnp.sum(jnp.where(lane == lane_idx, x, 0.0), axis=-1, keepdims=True)
    return jnp.broadcast_to(col, x.shape)


def _gdn_kernel(q_ref, k_ref, v_ref, z_ref, gt_ref, cwq_ref, cwk_ref, cwv_ref, alog_ref, dtb_ref,
                gn_ref, o_ref, qpad, kpad, vpad, q_scr, k_scr, w_scr, u_scr, g_scr, qk_scr, o_scr,
                state):
    ts = q_ref.shape[1]
    c = CHUNK
    hd = HEAD_DIM
    _load_conv_tile(q_ref, qpad, ts)
    _load_conv_tile(k_ref, kpad, ts)
    _load_conv_tile(v_ref, vpad, ts)

    @pl.when(pl.program_id(1) == 0)
    def _():
        state[...] = jnp.zeros(state.shape, F32)

    qc = _silu(_conv_from_pad(qpad, cwq_ref[...], ts))
    kc = _silu(_conv_from_pad(kpad, cwk_ref[...], ts))
    vc = _silu(_conv_from_pad(vpad, cwv_ref[...], ts))
    gates = gt_ref[0]
    neg_a = -jnp.exp(alog_ref[...])
    g_all = neg_a * _softplus(gates + dtb_ref[...])
    beta_all = _sigmoid(gates)
    for h in range(N_HEADS):
        hs = slice(h * hd, (h + 1) * hd)
        qh = qc[:, hs]
        kh = kc[:, hs]
        qn = qh * lax.rsqrt(jnp.sum(qh * qh, axis=-1, keepdims=True) + EPS) * (hd ** -0.5)
        kn = kh * lax.rsqrt(jnp.sum(kh * kh, axis=-1, keepdims=True) + EPS)
        beta = _lane_pick(beta_all, h)
        q_scr[:, hs] = qn
        k_scr[:, hs] = kn
        w_scr[:, hs] = kn * beta
        u_scr[:, hs] = vc[:, hs] * beta
        g_scr[:, hs] = _lane_pick(g_all, N_HEADS + h)

    incl = _tril(c)
    strict = _tril(c, strict=True)
    eye = (lax.broadcasted_iota(jnp.int32, (c, c), 0)
           == lax.broadcasted_iota(jnp.int32, (c, c), 1)).astype(F32)
    tri_f = incl.astype(F32)

    def prep(ci, carry):
        rows = pl.ds(pl.multiple_of(ci * c, c), c)
        gc_all = _dot_exact(tri_f, g_scr[rows, :])
        for h in range(N_HEADS):
            hs = slice(h * hd, (h + 1) * hd)
            gc = gc_all[:, hs]
            q = q_scr[rows, hs]
            k = k_scr[rows, hs]
            kb = w_scr[rows, hs]
            vb = u_scr[rows, hs]
            diff = gc[:, :c] - _row_form(gc)
            decay = jnp.exp(jnp.where(incl, diff, NEG))
            a_mat = jnp.where(strict, _dot_nt(kb, k) * decay, 0.0)
            t_inv = eye - a_mat
            a_pow = a_mat
            for _ in range(int(math.log2(c)) - 1):
                a_pow = _dot(a_pow, a_pow)
                t_inv = t_inv + _dot(t_inv, a_pow)
            e_gc = jnp.exp(gc)
            u_scr[rows, hs] = _dot(t_inv, vb)
            w_scr[rows, hs] = _dot(t_inv, kb * e_gc)
            qk_scr[rows, h * c:(h + 1) * c] = jnp.where(incl, _dot_nt(q, k) * decay, 0.0)
            q_scr[rows, hs] = q * e_gc
            gc_last = gc[c - 1:c, :]
            k_scr[rows, hs] = k * jnp.exp(gc_last - gc)
            g_scr[rows, hs] = jnp.broadcast_to(jnp.exp(gc_last), (c, hd))
        return carry

    lax.fori_loop(0, ts // c, prep, 0)

    def step(ci, carry):
        rows = pl.ds(pl.multiple_of(ci * c, c), c)
        for h in range(N_HEADS):
            hs = slice(h * hd, (h + 1) * hd)
            st = state[h]
            v_new = u_scr[rows, hs] - _dot(w_scr[rows, hs], st)
            o_scr[rows, hs] = _dot(q_scr[rows, hs], st) + _dot(qk_scr[rows, h * c:(h + 1) * c], v_new)
            g_last = g_scr[pl.ds(pl.multiple_of(ci * c, c), 1), hs]
            state[h] = st * g_last + _dot_tn(k_scr[rows, hs], v_new)
        return carry

    lax.fori_loop(0, ts // c, step, 0)
    y = _head_rms(o_scr[...], gn_ref[...]) * _silu(z_ref[0].astype(F32))
    o_ref[0] = y.astype(o_ref.dtype)


def gated_delta_mixer(proj, gates, conv_w, a_log_v, dt_bias_v, gn_c, ts=1024):
    b, s, _ = proj.shape
    w = GROUP_WIDTH

    def col(off):
        return pl.BlockSpec((1, ts, w), lambda bi, i, off=off: (bi, i, off))

    def cw(off):
        return pl.BlockSpec((CONV_WIDTH, w), lambda bi, i, off=off: (0, off))

    vec = pl.BlockSpec((1, GATE_LANES), lambda bi, i: (0, 0))
    big = lambda: pltpu.VMEM((ts, w), F32)
    return pl.pallas_call(
        _gdn_kernel,
        grid=(b, s // ts),
        in_specs=[col(COL_CQ), col(COL_CK), col(COL_CV), col(COL_CZ),
                  pl.BlockSpec((1, ts, GATE_LANES), lambda bi, i: (bi, i, 0)),
                  cw(0), cw(1), cw(2), vec, vec,
                  pl.BlockSpec((1, w), lambda bi, i: (0, 0))],
        out_specs=pl.BlockSpec((1, ts, w), lambda bi, i: (bi, i, 0)),
        out_shape=jax.ShapeDtypeStruct((b, s, w), BF16),
        scratch_shapes=[pltpu.VMEM((ts + SUBLANES, w), F32) for _ in range(3)]
        + [big(), big(), big(), big(), big(), pltpu.VMEM((ts, N_HEADS * CHUNK), F32), big(),
           pltpu.VMEM((N_HEADS, HEAD_DIM, HEAD_DIM), F32)],
        compiler_params=_cparams(("arbitrary", "arbitrary")),
        name="gated_delta_net",
    )(proj, proj, proj, proj, gates, conv_w, conv_w, conv_w, a_log_v, dt_bias_v, gn_c)


def _mlstm_kernel(q_ref, k_ref, v_ref, og_ref, gt_ref, ib_ref, fb_ref, gn_ref, o_ref,
                  ig_scr, lf_scr, h_scr, c_state, n_state, m_state):
    ts = q_ref.shape[1]
    c = CHUNK
    hd = HEAD_DIM

    @pl.when(pl.program_id(1) == 0)
    def _():
        c_state[...] = jnp.zeros(c_state.shape, F32)
        n_state[...] = jnp.zeros(n_state.shape, F32)
        m_state[...] = jnp.full(m_state.shape, M_INIT, F32)

    gates = gt_ref[0]
    i_all = gates + ib_ref[...]
    f_pre = gates + fb_ref[...]
    lf_all = jnp.minimum(f_pre, 0.0) - jnp.log1p(jnp.exp(-jnp.abs(f_pre)))
    for h in range(N_HEADS):
        hs = slice(h * hd, (h + 1) * hd)
        ig_scr[:, hs] = _lane_pick(i_all, 2 * N_HEADS + h)
        lf_scr[:, hs] = _lane_pick(lf_all, 3 * N_HEADS + h)

    incl = _tril(c)
    tri_f = incl.astype(F32)
    k_scale = 1.0 / math.sqrt(hd)

    def step(ci, carry):
        rows = pl.ds(pl.multiple_of(ci * c, c), c)
        b_all = _dot_exact(tri_f, lf_scr[rows, :])
        for h in range(N_HEADS):
            hs = slice(h * hd, (h + 1) * hd)
            q = q_ref[0, rows, hs]
            k = k_ref[0, rows, hs].astype(F32) * k_scale
            v = v_ref[0, rows, hs]
            bb = b_all[:, hs]
            ig = ig_scr[rows, hs]
            m_st = m_state[h]
            c_st = c_state[h]
            n_st = n_state[h]
            log_d = jnp.where(incl, bb[:, :c] + _row_form(ig - bb), NEG)
            inter = bb + m_st
            m_t = jnp.maximum(inter, jnp.max(log_d, axis=-1, keepdims=True))
            d_mat = jnp.exp(log_d - m_t[:, :c])
            e_inter = jnp.exp(inter - m_t)
            sc = _dot_nt(q, k) * d_mat
            num = e_inter * _dot(q, c_st) + _dot(sc, v)
            qn = jnp.sum(q.astype(F32) * n_st, axis=-1, keepdims=True)
            den = e_inter * qn + jnp.sum(sc, axis=-1, keepdims=True)
            h_scr[rows, hs] = num / jnp.maximum(jnp.abs(den), jnp.exp(-m_t))
            b_last = bb[c - 1:c, :]
            log_w = b_last - bb + ig
            m_new = jnp.maximum(b_last + m_st, jnp.max(log_w, axis=0, keepdims=True))
            wk = jnp.exp(log_w - m_new)
            keep = jnp.exp(b_last + m_st - m_new)
            kw = k * wk
            c_state[h] = keep * c_st + _dot_tn(kw, v)
            n_state[h] = keep * n_st + jnp.sum(kw, axis=0, keepdims=True)
            m_state[h] = m_new
        return carry

    lax.fori_loop(0, ts // c, step, 0)
    y = _head_rms(h_scr[...], gn_ref[...]) * _sigmoid(og_ref[0].astype(F32))
    o_ref[0] = y.astype(o_ref.dtype)


def mlstm_mixer(proj, gates, i_bias_v, f_bias_v, gn_d, ts=1024):
    b, s, _ = proj.shape
    w = GROUP_WIDTH

    def col(off):
        return pl.BlockSpec((1, ts, w), lambda bi, i, off=off: (bi, i, off))

    vec = pl.BlockSpec((1, GATE_LANES), lambda bi, i: (0, 0))
    big = lambda: pltpu.VMEM((ts, w), F32)
    return pl.pallas_call(
        _mlstm_kernel,
        grid=(b, s // ts),
        in_specs=[col(COL_DQ), col(COL_DK), col(COL_DV), col(COL_DO),
                  pl.BlockSpec((1, ts, GATE_LANES), lambda bi, i: (bi, i, 0)),
                  vec, vec, pl.BlockSpec((1, w), lambda bi, i: (0, 0))],
        out_specs=pl.BlockSpec((1, ts, w), lambda bi, i: (bi, i, 0)),
        out_shape=jax.ShapeDtypeStruct((b, s, w), BF16),
        scratch_shapes=[big(), big(), big(),
                        pltpu.VMEM((N_HEADS, HEAD_DIM, HEAD_DIM), F32),
                        pltpu.VMEM((N_HEADS, 1, HEAD_DIM), F32),
                        pltpu.VMEM((N_HEADS, 1, HEAD_DIM), F32)],
        compiler_params=_cparams(("arbitrary", "arbitrary")),
        name="mlstm",
    )(proj, proj, proj, proj, gates, i_bias_v, f_bias_v, gn_d)


def _router_kernel(x_ref, sc_ref, sh_ref, nw_ref, wr_ref, rb_ref, h_ref, g_ref):
    h = _modulated_norm(x_ref[0], nw_ref[...], sc_ref[0], sh_ref[0])
    h_ref[0] = h.astype(BF16)
    tm = h.shape[0]
    logits = lax.dot_general(wr_ref[...], h, (((1,), (1,)), ((), ())),
                             precision=HIGHEST, preferred_element_type=F32)
    scores = _sigmoid(logits)
    biased = scores + rb_ref[...][:, 0:1]
    sr = [scores[e:e + 1, :] for e in range(N_EXPERTS)]
    br = [biased[e:e + 1, :] for e in range(N_EXPERTS)]
    n_groups = N_EXPERTS // EXPERTS_PER_GROUP
    group_scores = []
    for g in range(n_groups):
        a, b, cc, d = br[4 * g:4 * g + 4]
        p, q = jnp.maximum(a, b), jnp.minimum(a, b)
        r, s = jnp.maximum(cc, d), jnp.minimum(cc, d)
        group_scores.append(jnp.maximum(p, r) + jnp.maximum(jnp.minimum(p, r), jnp.maximum(q, s)))
    sel = jnp.zeros((1, tm), jnp.int32)
    best = group_scores[0]
    for g in range(1, n_groups):
        better = group_scores[g] > best
        sel = jnp.where(better, g, sel)
        best = jnp.where(better, group_scores[g], best)
    masked = [jnp.where(sel == (e // EXPERTS_PER_GROUP), br[e], -jnp.inf) for e in range(N_EXPERTS)]

    def top1(vals):
        idx = jnp.zeros((1, tm), jnp.int32)
        bv = vals[0]
        for e in range(1, N_EXPERTS):
            better = vals[e] > bv
            idx = jnp.where(better, e, idx)
            bv = jnp.where(better, vals[e], bv)
        return idx

    i1 = top1(masked)
    i2 = top1([jnp.where(i1 == e, -jnp.inf, masked[e]) for e in range(N_EXPERTS)])
    s1 = sum(jnp.where(i1 == e, sr[e], 0.0) for e in range(N_EXPERTS))
    s2 = sum(jnp.where(i2 == e, sr[e], 0.0) for e in range(N_EXPERTS))
    tot = s1 + s2
    g_rows = [jnp.where(i1 == e, s1 / tot, 0.0) + jnp.where(i2 == e, s2 / tot, 0.0)
              for e in range(N_EXPERTS)]
    gates_t = jnp.concatenate(g_rows + [jnp.zeros((GATE_LANES - N_EXPERTS, tm), F32)], axis=0)
    g_ref[0] = gates_t.T


def router(x, scale, shift, norm_w, w_router_t, router_bias_col, tm=512):
    b, s, d = x.shape
    return pl.pallas_call(
        _router_kernel,
        grid=(b, s // tm),
        in_specs=[pl.BlockSpec((1, tm, d), lambda bi, i: (bi, i, 0)),
                  pl.BlockSpec((1, 1, d), lambda bi, i: (bi, 0, 0)),
                  pl.BlockSpec((1, 1, d), lambda bi, i: (bi, 0, 0)),
                  pl.BlockSpec((1, d), lambda bi, i: (0, 0)),
                  pl.BlockSpec((N_EXPERTS, d), lambda bi, i: (0, 0)),
                  pl.BlockSpec((N_EXPERTS, GATE_LANES), lambda bi, i: (0, 0))],
        out_specs=[pl.BlockSpec((1, tm, d), lambda bi, i: (bi, i, 0)),
                   pl.BlockSpec((1, tm, GATE_LANES), lambda bi, i: (bi, i, 0))],
        out_shape=[jax.ShapeDtypeStruct((b, s, d), BF16),
                   jax.ShapeDtypeStruct((b, s, GATE_LANES), F32)],
        compiler_params=_cparams(("arbitrary", "arbitrary")),
        name="router",
    )(x, scale, shift, norm_w, w_router_t, router_bias_col)


def _moe_kernel(x_ref, h_ref, g_ref, wg_ref, wu_ref, wd_ref, gate_ref, o_ref, acc):
    e = pl.program_id(2)

    @pl.when(e == 0)
    def _():
        acc[...] = jnp.zeros(acc.shape, F32)

    h = h_ref[0]
    hg = jnp.dot(h, wg_ref[0], preferred_element_type=F32)
    hu = jnp.dot(h, wu_ref[0], preferred_element_type=F32)
    gates = g_ref[0]
    lane = lax.broadcasted_iota(jnp.int32, gates.shape, 1)
    gcol = jnp.sum(jnp.where(lane == e, gates, 0.0), axis=-1, keepdims=True)
    act = _silu(hg) * hu * gcol
    acc[...] += _dot(act, wd_ref[0])

    @pl.when(e == pl.num_programs(2) - 1)
    def _():
        o_ref[0] = x_ref[0] + gate_ref[0] * acc[...]


def moe_dense(x, h, gates, w_gate, w_up, w_down, gate, tm=512):
    b, s, d = x.shape
    ne, _, dff = w_gate.shape
    return pl.pallas_call(
        _moe_kernel,
        grid=(b, s // tm, ne),
        in_specs=[pl.BlockSpec((1, tm, d), lambda bi, i, e: (bi, i, 0)),
                  pl.BlockSpec((1, tm, d), lambda bi, i, e: (bi, i, 0)),
                  pl.BlockSpec((1, tm, GATE_LANES), lambda bi, i, e: (bi, i, 0)),
                  pl.BlockSpec((1, d, dff), lambda bi, i, e: (e, 0, 0)),
                  pl.BlockSpec((1, d, dff), lambda bi, i, e: (e, 0, 0)),
                  pl.BlockSpec((1, dff, d), lambda bi, i, e: (e, 0, 0)),
                  pl.BlockSpec((1, 1, d), lambda bi, i, e: (bi, 0, 0))],
        out_specs=pl.BlockSpec((1, tm, d), lambda bi, i, e: (bi, i, 0)),
        out_shape=jax.ShapeDtypeStruct((b, s, d), F32),
        scratch_shapes=[pltpu.VMEM((tm, d), F32)],
        compiler_params=_cparams(("arbitrary", "arbitrary", "arbitrary")),
        name="moe_dense",
    )(x, h, gates, w_gate, w_up, w_down, gate)


def _final_norm_kernel(x_ref, nw_ref, o_ref):
    x = x_ref[0]
    ms = jnp.mean(x * x, axis=-1, keepdims=True)
    o_ref[0] = x * lax.rsqrt(ms + EPS) * nw_ref[...]


def final_norm(x, norm_w, tm=1024):
    b, s, d = x.shape
    return pl.pallas_call(
        _final_norm_kernel,
        grid=(b, s // tm),
        in_specs=[pl.BlockSpec((1, tm, d), lambda bi, i: (bi, i, 0)),
                  pl.BlockSpec((1, d), lambda bi, i: (0, 0))],
        out_specs=pl.BlockSpec((1, tm, d), lambda bi, i: (bi, i, 0)),
        out_shape=jax.ShapeDtypeStruct((b, s, d), F32),
        compiler_params=_cparams(("arbitrary", "arbitrary")),
        name="final_norm",
    )(x, norm_w)


def _split_w_in(w_in):
    gw = GROUP_WIDTH
    a_end = 3 * gw
    b_end = a_end + 2 * gw
    c_main_end = b_end + 4 * gw
    c_end = c_main_end + 2 * N_HEADS
    d_main_end = c_end + 4 * gw
    main = jnp.concatenate([w_in[:, :c_main_end], w_in[:, c_end:d_main_end]], axis=1)
    small = jnp.concatenate([w_in[:, c_main_end:c_end], w_in[:, d_main_end:]], axis=1)
    small = jnp.pad(small, ((0, 0), (0, GATE_LANES - small.shape[1])))
    return main.astype(BF16), small.astype(BF16)


def _lanes(vec, offset):
    return jnp.zeros((1, GATE_LANES), F32).at[0, offset:offset + N_HEADS].set(vec)


def _rope_tables(s):
    half = HEAD_DIM // 2
    inv_freq = ROPE_THETA ** (-jnp.arange(half, dtype=F32) / half)
    ang = jnp.arange(s, dtype=F32)[:, None] * inv_freq[None, :]
    cos, sin = jnp.cos(ang), jnp.sin(ang)
    return jnp.concatenate([cos, cos], axis=1), jnp.concatenate([-sin, sin], axis=1)


def hybrid_mixer_layer(x, sc1, sh1, g1, norm_w, w_in, w_out, gn_a, conv_b_w, conv_b_b, w_rg, b_rg,
                       w_ig, b_ig, lru_lambda, gn_b, conv_c_w, gdn_a_log, gdn_dt_bias, gdn_norm,
                       mlstm_i_bias, mlstm_f_bias, gn_d, rope):
    w_main, w_small = _split_w_in(w_in)
    proj, gates = in_projection(x, sc1, sh1, norm_w.reshape(1, -1), w_main, w_small)
    row = lambda v: v.reshape(1, -1)
    y_a = dilated_attention_mixer(proj, rope[0], rope[1], row(gn_a))
    y_b = rg_lru_mixer(proj, conv_b_w, row(conv_b_b), w_rg.astype(BF16), row(b_rg),
                       w_ig.astype(BF16), row(b_ig), row(lru_lambda), row(gn_b))
    y_c = gated_delta_mixer(proj, gates, conv_c_w, _lanes(gdn_a_log, N_HEADS),
                            _lanes(gdn_dt_bias, N_HEADS), row(jnp.tile(gdn_norm, N_HEADS)))
    y_d = mlstm_mixer(proj, gates, _lanes(mlstm_i_bias, 2 * N_HEADS),
                      _lanes(mlstm_f_bias, 3 * N_HEADS), row(gn_d))
    return out_projection(x, (y_a, y_b, y_c, y_d), w_out.astype(BF16), g1)


def moe_layer(x, sc2, sh2, g2, norm_w, w_router, router_bias, w_gate, w_up, w_down):
    rb = jnp.broadcast_to(router_bias.reshape(-1, 1), (N_EXPERTS, GATE_LANES))
    h, gates = router(x, sc2, sh2, norm_w.reshape(1, -1), w_router.T, rb)
    return moe_dense(x, h, gates, w_gate.astype(BF16), w_up.astype(BF16), w_down.astype(BF16), g2)


def kernel(x, c, norm_mix, norm_ffn, norm_final, w_ada, b_ada, w_in, w_out, gn_a, conv_b_w, conv_b_b, w_rg, b_rg, w_ig, b_ig, lru_lambda, gn_b, conv_c_w, gdn_a_log, gdn_dt_bias, gdn_norm, mlstm_i_bias, mlstm_f_bias, gn_d, w_router, router_bias, moe_w_gate, moe_w_up, moe_w_down):
    depth = w_ada.shape[0]
    b, s, d = x.shape
    mod = ada_modulation(c, w_ada, b_ada)
    rope = _rope_tables(s)
    for l in range(depth):
        sh1, sc1, g1, sh2, sc2, g2 = [mod[l, :, None, i * d:(i + 1) * d] for i in range(6)]
        x = hybrid_mixer_layer(x, sc1, sh1, g1, norm_mix[l], w_in[l], w_out[l], gn_a[l],
                               conv_b_w[l], conv_b_b[l], w_rg[l], b_rg[l], w_ig[l], b_ig[l],
                               lru_lambda[l], gn_b[l], conv_c_w[l], gdn_a_log[l], gdn_dt_bias[l],
                               gdn_norm[l], mlstm_i_bias[l], mlstm_f_bias[l], gn_d[l], rope)
        x = moe_layer(x, sc2, sh2, g2, norm_ffn[l], w_router, router_bias,
                      moe_w_gate[l], moe_w_up[l], moe_w_down[l])
    return final_norm(x, norm_final.reshape(1, -1))
```

```python
import functools
import math

import jax
import jax.numpy as jnp
from jax import lax
from jax.experimental import pallas as pl
from jax.experimental.pallas import tpu as pltpu

F32 = jnp.float32
BF16 = jnp.bfloat16
HIGHEST = lax.Precision.HIGHEST

HEAD_DIM = 128
N_HEADS = 4
GROUP_WIDTH = N_HEADS * HEAD_DIM
DILATED_PAIRS = ((128, 1), (512, 4), (2048, 16))
ATTN_BLOCK = 128
ROPE_THETA = 10000.0
CONV_WIDTH = 4
LRU_C = 8.0
CHUNK = 64
N_EXPERTS = 16
EXPERTS_PER_GROUP = 4
EPS = 1e-6
M_INIT = -1e30
NEG = -1e30
SUBLANES = 8
VMEM_LIMIT = 56 * 1024 * 1024

COL_AQ, COL_AK, COL_AV = 0, 4, 8
COL_BX, COL_BG = 3, 4
COL_CQ, COL_CK, COL_CV, COL_CZ = 5, 6, 7, 8
COL_DQ, COL_DK, COL_DV, COL_DO = 9, 10, 11, 12
MAIN_COLS = 13 * GROUP_WIDTH
GATE_LANES = 128


def _cparams(sem):
    return pltpu.CompilerParams(dimension_semantics=sem, vmem_limit_bytes=VMEM_LIMIT)


def _dot(a, b):
    return jnp.dot(a.astype(BF16), b.astype(BF16), preferred_element_type=F32)


def _dot_nt(a, b):
    return lax.dot_general(a.astype(BF16), b.astype(BF16), (((1,), (1,)), ((), ())),
                           preferred_element_type=F32)


def _dot_tn(a, b):
    return lax.dot_general(a.astype(BF16), b.astype(BF16), (((0,), (0,)), ((), ())),
                           preferred_element_type=F32)


def _dot_exact(a, b):
    return jnp.dot(a, b, precision=HIGHEST, preferred_element_type=F32)


def _row_form(col_b):
    c = col_b.shape[0]
    lane = lax.broadcasted_iota(jnp.int32, col_b.shape, 1)
    picked = jnp.where(lane == 0, col_b, 0.0)
    return lax.dot_general(jnp.ones((c, col_b.shape[1]), F32), picked, (((1,), (1,)), ((), ())),
                           precision=HIGHEST, preferred_element_type=F32)


def _softplus(z):
    return jnp.maximum(z, 0.0) + jnp.log1p(jnp.exp(-jnp.abs(z)))


def _sigmoid(z):
    return 1.0 / (1.0 + jnp.exp(-z))


def _silu(z):
    return z * _sigmoid(z)


def _head_rms(x, gain):
    outs = []
    for g in range(x.shape[1] // HEAD_DIM):
        xg = x[:, g * HEAD_DIM:(g + 1) * HEAD_DIM]
        ms = jnp.mean(xg * xg, axis=-1, keepdims=True)
        outs.append(xg * lax.rsqrt(ms + EPS))
    y = outs[0] if len(outs) == 1 else jnp.concatenate(outs, axis=1)
    return y * gain


def _tril(c, strict=False):
    row = lax.broadcasted_iota(jnp.int32, (c, c), 0)
    col = lax.broadcasted_iota(jnp.int32, (c, c), 1)
    return (col < row) if strict else (col <= row)


def _ada_kernel(c_ref, w_ref, b_ref, o_ref):
    c = c_ref[...]
    o_ref[0] = _dot(_silu(c), w_ref[0]) + b_ref[0]


def ada_modulation(c, w_ada, b_ada):
    depth, d, n = w_ada.shape
    b = c.shape[0]
    rows = -(-b // SUBLANES) * SUBLANES
    c_pad = jnp.zeros((rows, d), F32).at[:b].set(c)
    tn = 1024
    out = pl.pallas_call(
        _ada_kernel,
        grid=(depth, n // tn),
        in_specs=[pl.BlockSpec((rows, d), lambda l, j: (0, 0)),
                  pl.BlockSpec((1, d, tn), lambda l, j: (l, 0, j)),
                  pl.BlockSpec((1, 1, tn), lambda l, j: (l, 0, j))],
        out_specs=pl.BlockSpec((1, rows, tn), lambda l, j: (l, 0, j)),
        out_shape=jax.ShapeDtypeStruct((depth, rows, n), F32),
        compiler_params=_cparams(("arbitrary", "arbitrary")),
        name="ada_modulation",
    )(c_pad, w_ada, b_ada.reshape(depth, 1, n))
    return out[:, :b]


def _modulated_norm(x, nw, sc, sh):
    ms = jnp.mean(x * x, axis=-1, keepdims=True)
    return (x * lax.rsqrt(ms + EPS) * nw) * (1.0 + sc) + sh


def _in_proj_kernel(x_ref, sc_ref, sh_ref, nw_ref, w_ref, wg_ref, proj_ref, gates_ref, h_scr):
    @pl.when(pl.program_id(2) == 0)
    def _():
        h = _modulated_norm(x_ref[0], nw_ref[...], sc_ref[0], sh_ref[0]).astype(BF16)
        h_scr[...] = h
        gates_ref[0] = jnp.dot(h, wg_ref[...], preferred_element_type=F32)

    proj_ref[0] = jnp.dot(h_scr[...], w_ref[...], preferred_element_type=F32).astype(BF16)


def in_projection(x, scale, shift, norm_w, w_main, w_gate, tm=512, tn=1664):
    b, s, d = x.shape
    n = w_main.shape[1]
    return pl.pallas_call(
        _in_proj_kernel,
        grid=(b, s // tm, n // tn),
        in_specs=[pl.BlockSpec((1, tm, d), lambda bi, i, j: (bi, i, 0)),
                  pl.BlockSpec((1, 1, d), lambda bi, i, j: (bi, 0, 0)),
                  pl.BlockSpec((1, 1, d), lambda bi, i, j: (bi, 0, 0)),
                  pl.BlockSpec((1, d), lambda bi, i, j: (0, 0)),
                  pl.BlockSpec((d, tn), lambda bi, i, j: (0, j)),
                  pl.BlockSpec((d, GATE_LANES), lambda bi, i, j: (0, 0))],
        out_specs=[pl.BlockSpec((1, tm, tn), lambda bi, i, j: (bi, i, j)),
                   pl.BlockSpec((1, tm, GATE_LANES), lambda bi, i, j: (bi, i, 0))],
        out_shape=[jax.ShapeDtypeStruct((b, s, n), BF16),
                   jax.ShapeDtypeStruct((b, s, GATE_LANES), F32)],
        scratch_shapes=[pltpu.VMEM((tm, d), BF16)],
        compiler_params=_cparams(("arbitrary", "arbitrary", "arbitrary")),
        name="in_projection",
    )(x, scale, shift, norm_w, w_main, w_gate)


def _out_proj_kernel(x_ref, ya_ref, yb_ref, yc_ref, yd_ref, w_ref, g_ref, o_ref):
    acc = None
    for k, y_ref in enumerate((ya_ref, yb_ref, yc_ref, yd_ref)):
        part = jnp.dot(y_ref[0], w_ref[k * GROUP_WIDTH:(k + 1) * GROUP_WIDTH, :],
                       preferred_element_type=F32)
        acc = part if acc is None else acc + part
    o_ref[0] = x_ref[0] + g_ref[0] * acc


def out_projection(x, ys, w_out, gate, tm=512):
    b, s, d = x.shape
    yspec = pl.BlockSpec((1, tm, GROUP_WIDTH), lambda bi, i: (bi, i, 0))
    return pl.pallas_call(
        _out_proj_kernel,
        grid=(b, s // tm),
        in_specs=[pl.BlockSpec((1, tm, d), lambda bi, i: (bi, i, 0)),
                  yspec, yspec, yspec, yspec,
                  pl.BlockSpec(w_out.shape, lambda bi, i: (0, 0)),
                  pl.BlockSpec((1, 1, d), lambda bi, i: (bi, 0, 0))],
        out_specs=pl.BlockSpec((1, tm, d), lambda bi, i: (bi, i, 0)),
        out_shape=jax.ShapeDtypeStruct((b, s, d), F32),
        compiler_params=_cparams(("arbitrary", "arbitrary")),
        name="out_projection",
    )(x, *ys, w_out, gate)


def _attn_kernel(q_ref, k_ref, v_ref, cos_ref, sin_ref, gn_ref, o_ref, qs, ks, vs, acc, ms, ls):
    s = q_ref.shape[1]
    blk = ATTN_BLOCK
    cos = cos_ref[...]
    sin = sin_ref[...]

    def rot(t):
        return t * cos + pltpu.roll(t, HEAD_DIM // 2, 1) * sin

    qs[...] = rot(q_ref[0].astype(F32)) * (1.0 / math.sqrt(HEAD_DIM))
    ks[...] = rot(k_ref[0].astype(F32))
    vs[...] = v_ref[0].astype(F32)

    row = lax.broadcasted_iota(jnp.int32, (blk, 2 * blk), 0)
    col = lax.broadcasted_iota(jnp.int32, (blk, 2 * blk), 1)
    own_ok = (col >= blk) & (col - blk <= row)

    for branch, (window, dil) in enumerate(DILATED_PAIRS):
        nb = s // (blk * dil)

        def rows(start, dil=dil):
            if dil == 1:
                return pl.ds(pl.multiple_of(start, blk), blk)
            return pl.ds(start, blk, stride=dil)

        def body(idx, carry, dil=dil, nb=nb, branch=branch, rows=rows):
            r = idx // nb
            n = idx % nb
            start = r + n * (blk * dil)
            pstart = jnp.where(n > 0, start - blk * dil, start)
            qb = qs[rows(start), :]
            kc = jnp.concatenate([ks[rows(pstart), :], ks[rows(start), :]], axis=0)
            vc = jnp.concatenate([vs[rows(pstart), :], vs[rows(start), :]], axis=0)
            sc = _dot_nt(qb, kc)
            first = jnp.where(n > 0, 0, 4 * blk)
            valid = own_ok | ((col < blk) & (col >= row + first))
            sc = jnp.where(valid, sc, NEG)
            mb = jnp.max(sc, axis=-1, keepdims=True)
            p = jnp.exp(sc - mb)
            den = jnp.sum(p, axis=-1, keepdims=True)
            num = _dot(p, vc)
            mb = jnp.broadcast_to(mb, (blk, HEAD_DIM))
            den = jnp.broadcast_to(den, (blk, HEAD_DIM))
            if branch == 0:
                acc[rows(start), :] = num
                ms[rows(start), :] = mb
                ls[rows(start), :] = den
            else:
                m_old = ms[rows(start), :]
                m_new = jnp.maximum(m_old, mb)
                a_old = jnp.exp(m_old - m_new)
                a_blk = jnp.exp(mb - m_new)
                acc[rows(start), :] = acc[rows(start), :] * a_old + num * a_blk
                ls[rows(start), :] = ls[rows(start), :] * a_old + den * a_blk
                ms[rows(start), :] = m_new
            return carry

        lax.fori_loop(0, s // blk, body, 0, unroll=8)

    out = acc[...] / ls[...]
    o_ref[0] = _head_rms(out, gn_ref[...]).astype(o_ref.dtype)


def dilated_attention_mixer(proj, cos2, sin2, gn_a):
    b, s, _ = proj.shape
    hd = HEAD_DIM

    def col(off):
        return pl.BlockSpec((1, s, hd), lambda bi, h, off=off: (bi, 0, off + h))

    return pl.pallas_call(
        _attn_kernel,
        grid=(b, N_HEADS),
        in_specs=[col(COL_AQ), col(COL_AK), col(COL_AV),
                  pl.BlockSpec((s, hd), lambda bi, h: (0, 0)),
                  pl.BlockSpec((s, hd), lambda bi, h: (0, 0)),
                  pl.BlockSpec((1, hd), lambda bi, h: (0, h))],
        out_specs=pl.BlockSpec((1, s, hd), lambda bi, h: (bi, 0, h)),
        out_shape=jax.ShapeDtypeStruct((b, s, GROUP_WIDTH), BF16),
        scratch_shapes=[pltpu.VMEM((s, hd), F32) for _ in range(6)],
        compiler_params=_cparams(("arbitrary", "arbitrary")),
        name="dilated_attention",
    )(proj, proj, proj, cos2, sin2, gn_a)


def _load_conv_tile(x_ref, xpad, ts):
    @pl.when(pl.program_id(1) == 0)
    def _():
        xpad[0:SUBLANES, :] = jnp.zeros((SUBLANES, xpad.shape[1]), F32)

    @pl.when(pl.program_id(1) > 0)
    def _():
        xpad[0:SUBLANES, :] = xpad[ts:ts + SUBLANES, :]

    xpad[SUBLANES:SUBLANES + ts, :] = x_ref[0].astype(F32)


def _conv_from_pad(xpad, w, ts):
    y = None
    for j in range(CONV_WIDTH):
        off = SUBLANES - (CONV_WIDTH - 1) + j
        term = xpad[off:off + ts, :] * w[j:j + 1, :]
        y = term if y is None else y + term
    return y


def _lru_kernel(x_ref, g_ref, cw_ref, cb_ref, wr_ref, br_ref, wi_ref, bi_ref, lam_ref, gn_ref,
                o_ref, xpad, a_scr, u_scr, carry):
    ts = x_ref.shape[1]
    w = x_ref.shape[2]
    _load_conv_tile(x_ref, xpad, ts)

    @pl.when(pl.program_id(1) == 0)
    def _():
        carry[...] = jnp.zeros(carry.shape, F32)

    xc = _conv_from_pad(xpad, cw_ref[...], ts) + cb_ref[...]
    r_parts, i_parts = [], []
    for g in range(N_HEADS):
        xg = xc[:, g * HEAD_DIM:(g + 1) * HEAD_DIM]
        r_parts.append(_dot(xg, wr_ref[g]))
        i_parts.append(_dot(xg, wi_ref[g]))
    r = _sigmoid(jnp.concatenate(r_parts, axis=1) + br_ref[...])
    ig = _sigmoid(jnp.concatenate(i_parts, axis=1) + bi_ref[...])
    log_a = (-LRU_C) * r * _softplus(-lam_ref[...])
    a_scr[...] = jnp.exp(log_a)
    u_scr[...] = jnp.sqrt(1.0 - jnp.exp(2.0 * log_a)) * (ig * xc)

    sub = lax.broadcasted_iota(jnp.int32, (SUBLANES, w), 0)

    def body(j, h_prev):
        rows = pl.ds(pl.multiple_of(j * SUBLANES, SUBLANES), SUBLANES)
        a = a_scr[rows, :]
        u = u_scr[rows, :]
        for sh in (1, 2, 4):
            keep = sub >= sh
            a_sh = jnp.where(keep, pltpu.roll(a, sh, 0), 1.0)
            u_sh = jnp.where(keep, pltpu.roll(u, sh, 0), 0.0)
            u = a * u_sh + u
            a = a * a_sh
        h = u + a * h_prev
        u_scr[rows, :] = h
        return jnp.broadcast_to(h[SUBLANES - 1:SUBLANES, :], (SUBLANES, w))

    carry[...] = lax.fori_loop(0, ts // SUBLANES, body, carry[...])
    y = _head_rms(u_scr[...], gn_ref[...]) * jax.nn.gelu(g_ref[0].astype(F32))
    o_ref[0] = y.astype(o_ref.dtype)


def rg_lru_mixer(proj, conv_w, conv_b, w_rg, b_rg, w_ig, b_ig, lam, gn_b, ts=1024):
    b, s, _ = proj.shape
    w = GROUP_WIDTH
    vec = pl.BlockSpec((1, w), lambda bi, i: (0, 0))
    blockdiag = pl.BlockSpec((N_HEADS, HEAD_DIM, HEAD_DIM), lambda bi, i: (0, 0, 0))
    return pl.pallas_call(
        _lru_kernel,
        grid=(b, s // ts),
        in_specs=[pl.BlockSpec((1, ts, w), lambda bi, i: (bi, i, COL_BX)),
                  pl.BlockSpec((1, ts, w), lambda bi, i: (bi, i, COL_BG)),
                  pl.BlockSpec((CONV_WIDTH, w), lambda bi, i: (0, 0)),
                  vec, blockdiag, vec, blockdiag, vec, vec, vec],
        out_specs=pl.BlockSpec((1, ts, w), lambda bi, i: (bi, i, 0)),
        out_shape=jax.ShapeDtypeStruct((b, s, w), BF16),
        scratch_shapes=[pltpu.VMEM((ts + SUBLANES, w), F32), pltpu.VMEM((ts, w), F32),
                        pltpu.VMEM((ts, w), F32), pltpu.VMEM((SUBLANES, w), F32)],
        compiler_params=_cparams(("arbitrary", "arbitrary")),
        name="rg_lru",
    )(proj, proj, conv_w, conv_b, w_rg, b_rg, w_ig, b_ig, lam, gn_b)


def _lane_pick(x, lane_idx):
    lane = lax.broadcasted_iota(jnp.int32, x.shape, 1)
    col = jnp.sum(jnp.where(lane == lane_idx, x, 0.0), axis=-1, keepdims=True)
    return jnp.broadcast_to(col, x.shape)


def _bdot(a, b):
    return jnp.einsum("nij,njk->nik", a.astype(BF16), b.astype(BF16), preferred_element_type=F32)


def _bdot_nt(a, b):
    return jnp.einsum("nid,njd->nij", a.astype(BF16), b.astype(BF16), preferred_element_type=F32)


def _bdot_tn(a, b):
    return jnp.einsum("ncd,nce->nde", a.astype(BF16), b.astype(BF16), preferred_element_type=F32)


def _chunk_cumsum(x, c):
    nc = x.shape[0] // c
    tri = jnp.broadcast_to(_tril(c).astype(F32), (nc, c, c))
    return jnp.einsum("nij,njk->nik", tri, x.reshape(nc, c, x.shape[1]), precision=HIGHEST,
                      preferred_element_type=F32)


def _row_forms(x3, specs):
    nc, c, w = x3.shape
    rows = len(specs) * c
    lane = lax.broadcasted_iota(jnp.int32, (rows, w), 1)
    row = lax.broadcasted_iota(jnp.int32, (rows, w), 0)
    sel = jnp.zeros((rows, w), F32)
    for i, spec in enumerate(specs):
        in_rows = (row >= i * c) & (row < (i + 1) * c)
        for ln, coeff in spec:
            sel = jnp.where(in_rows & (lane == ln), coeff, sel)
    out = jnp.einsum("nil,njl->nij", jnp.broadcast_to(sel, (nc, rows, w)), x3, precision=HIGHEST,
                     preferred_element_type=F32)
    return [out[:, i * c:(i + 1) * c, :] for i in range(len(specs))]


def _lane_pick3(x3, lane_idx):
    lane = lax.broadcasted_iota(jnp.int32, x3.shape, 2)
    col = jnp.sum(jnp.where(lane == lane_idx, x3, 0.0), axis=-1, keepdims=True)
    return jnp.broadcast_to(col, x3.shape[:2] + (HEAD_DIM,))


def _gdn_kernel(q_ref, k_ref, v_ref, z_ref, gt_ref, cwq_ref, cwk_ref, cwv_ref, alog_ref, dtb_ref,
                gn_ref, o_ref, qpad, kpad, vpad, p_scr, n_scr, qp_scr, op_scr, gl_scr, o_scr, state):
    ts = q_ref.shape[1]
    c = CHUNK
    hd = HEAD_DIM
    nc = ts // c
    _load_conv_tile(q_ref, qpad, ts)
    _load_conv_tile(k_ref, kpad, ts)
    _load_conv_tile(v_ref, vpad, ts)

    @pl.when(pl.program_id(1) == 0)
    def _():
        state[...] = jnp.zeros(state.shape, F32)

    qc = _silu(_conv_from_pad(qpad, cwq_ref[...], ts))
    kc = _silu(_conv_from_pad(kpad, cwk_ref[...], ts))
    vc = _silu(_conv_from_pad(vpad, cwv_ref[...], ts))
    gates = gt_ref[0]
    neg_a = -jnp.exp(alog_ref[...])
    g_all = neg_a * _softplus(gates + dtb_ref[...])
    beta3_all = _sigmoid(gates).reshape(nc, c, GATE_LANES)
    gc3_all = _chunk_cumsum(g_all, c)
    gc_rows = _row_forms(gc3_all, [[(N_HEADS + h, 1.0)] for h in range(N_HEADS)])
    incl = _tril(c)
    strict = _tril(c, strict=True)
    eye = (lax.broadcasted_iota(jnp.int32, (c, c), 0)
           == lax.broadcasted_iota(jnp.int32, (c, c), 1)).astype(F32)

    q_u, k_u, kb_u, vb_u, kbe_u, qd_u, kd_u, dec_u, gl_u = ([] for _ in range(9))
    for h in range(N_HEADS):
        hs = slice(h * hd, (h + 1) * hd)
        qh = qc[:, hs]
        kh = kc[:, hs]
        qn = (qh * lax.rsqrt(jnp.sum(qh * qh, axis=-1, keepdims=True) + EPS)
              * (hd ** -0.5)).reshape(nc, c, hd)
        kn = (kh * lax.rsqrt(jnp.sum(kh * kh, axis=-1, keepdims=True) + EPS)).reshape(nc, c, hd)
        beta = _lane_pick3(beta3_all, h)
        gc = _lane_pick3(gc3_all, N_HEADS + h)
        e_gc = jnp.exp(gc)
        gc_last = gc[:, c - 1:c, :]
        kb = kn * beta
        q_u.append(qn.astype(BF16))
        k_u.append(kn.astype(BF16))
        kb_u.append(kb.astype(BF16))
        vb_u.append((vc[:, hs].reshape(nc, c, hd) * beta).astype(BF16))
        kbe_u.append((kb * e_gc).astype(BF16))
        qd_u.append(qn * e_gc)
        kd_u.append((kn * jnp.exp(gc_last - gc)).astype(BF16))
        dec_u.append(jnp.exp(jnp.where(incl, gc[:, :, :c] - gc_rows[h], NEG)))
        gl_u.append(jnp.exp(gc_last))
    cat = lambda parts: jnp.concatenate(parts, axis=0)
    q_u, k_u, kb_u, vb_u, kbe_u, qd_u, kd_u, dec_u = (
        cat(t) for t in (q_u, k_u, kb_u, vb_u, kbe_u, qd_u, kd_u, dec_u))
    gl_scr[...] = cat(gl_u)

    a_mat = jnp.where(strict, _bdot_nt(kb_u, k_u) * dec_u, 0.0)
    t_inv = eye - a_mat
    a_pow = a_mat
    for _ in range(int(math.log2(c)) - 1):
        a_pow = _bdot(a_pow, a_pow)
        t_inv = t_inv + _bdot(t_inv, a_pow)
    u = _bdot(t_inv, vb_u)
    w = _bdot(t_inv, kbe_u)
    qk = jnp.where(incl, _bdot_nt(q_u, k_u) * dec_u, 0.0)
    p_scr[...] = _bdot_tn(kd_u, w).astype(BF16)
    n_scr[...] = _bdot_tn(kd_u, u)
    qp_scr[...] = (qd_u - _bdot(qk, w)).astype(BF16)
    op_scr[...] = _bdot(qk, u)

    def step(ci, carry):
        rows = pl.ds(pl.multiple_of(ci * c, c), c)
        for h in range(N_HEADS):
            hs = slice(h * hd, (h + 1) * hd)
            idx = h * nc + ci
            st = state[h]
            stb = st.astype(BF16)
            o_scr[rows, hs] = jnp.dot(qp_scr[idx], stb, preferred_element_type=F32) + op_scr[idx]
            state[h] = (gl_scr[idx] * st - jnp.dot(p_scr[idx], stb, preferred_element_type=F32)
                        + n_scr[idx])
        return carry

    lax.fori_loop(0, nc, step, 0)
    y = _head_rms(o_scr[...], gn_ref[...]) * _silu(z_ref[0].astype(F32))
    o_ref[0] = y.astype(o_ref.dtype)


def gated_delta_mixer(proj, gates, conv_w, a_log_v, dt_bias_v, gn_c, ts=512):
    b, s, _ = proj.shape
    w = GROUP_WIDTH
    hd = HEAD_DIM
    units = N_HEADS * (ts // CHUNK)

    def col(off):
        return pl.BlockSpec((1, ts, w), lambda bi, i, off=off: (bi, i, off))

    def cw(off):
        return pl.BlockSpec((CONV_WIDTH, w), lambda bi, i, off=off: (0, off))

    vec = pl.BlockSpec((1, GATE_LANES), lambda bi, i: (0, 0))
    big = lambda: pltpu.VMEM((ts, w), F32)
    return pl.pallas_call(
        _gdn_kernel,
        grid=(b, s // ts),
        in_specs=[col(COL_CQ), col(COL_CK), col(COL_CV), col(COL_CZ),
                  pl.BlockSpec((1, ts, GATE_LANES), lambda bi, i: (bi, i, 0)),
                  cw(0), cw(1), cw(2), vec, vec,
                  pl.BlockSpec((1, w), lambda bi, i: (0, 0))],
        out_specs=pl.BlockSpec((1, ts, w), lambda bi, i: (bi, i, 0)),
        out_shape=jax.ShapeDtypeStruct((b, s, w), BF16),
        scratch_shapes=[pltpu.VMEM((ts + SUBLANES, w), F32) for _ in range(3)]
        + [pltpu.VMEM((units, hd, hd), BF16), pltpu.VMEM((units, hd, hd), F32),
           pltpu.VMEM((units, CHUNK, hd), BF16), pltpu.VMEM((units, CHUNK, hd), F32),
           pltpu.VMEM((units, 1, hd), F32), big(),
           pltpu.VMEM((N_HEADS, hd, hd), F32)],
        compiler_params=_cparams(("arbitrary", "arbitrary")),
        name="gated_delta_net",
    )(proj, proj, proj, proj, gates, conv_w, conv_w, conv_w, a_log_v, dt_bias_v, gn_c)


def _mlstm_kernel(q_ref, k_ref, v_ref, og_ref, gt_ref, ib_ref, fb_ref, gn_ref, o_ref,
                  kvn_scr, keep_scr, qcn_scr, h_scr, cn_state, m_state):
    ts = q_ref.shape[1]
    c = CHUNK
    hd = HEAD_DIM
    nc = ts // c

    @pl.when(pl.program_id(1) == 0)
    def _():
        cn_state[...] = jnp.zeros(cn_state.shape, F32)
        m_state[...] = jnp.full(m_state.shape, M_INIT, F32)

    gates = gt_ref[0]
    i_all = gates + ib_ref[...]
    f_pre = gates + fb_ref[...]
    lf_all = jnp.minimum(f_pre, 0.0) - jnp.log1p(jnp.exp(-jnp.abs(f_pre)))
    b3_all = _chunk_cumsum(lf_all, c)
    lane = lax.broadcasted_iota(jnp.int32, (nc, c, GATE_LANES), 2)
    ib3_all = jnp.where(lane < 3 * N_HEADS, i_all.reshape(nc, c, GATE_LANES), b3_all)
    ib_rows = _row_forms(ib3_all, [[(2 * N_HEADS + h, 1.0), (3 * N_HEADS + h, -1.0)]
                                   for h in range(N_HEADS)])
    incl = _tril(c)
    k_scale = 1.0 / math.sqrt(hd)
    ones_v = jnp.ones((nc, c, hd), BF16)

    per_head = []
    for h in range(N_HEADS):
        hs = slice(h * hd, (h + 1) * hd)
        q3 = q_ref[0, :, hs].reshape(nc, c, hd)
        k3 = k_ref[0, :, hs].astype(F32).reshape(nc, c, hd) * k_scale
        v3 = v_ref[0, :, hs].reshape(nc, c, hd)
        bb = _lane_pick3(b3_all, 3 * N_HEADS + h)
        ig = _lane_pick3(ib3_all, 2 * N_HEADS + h)
        log_d = jnp.where(incl, bb[:, :, :c] + ib_rows[h], NEG)
        max_ld = jnp.max(log_d, axis=-1, keepdims=True)
        b_last = bb[:, c - 1:c, :]
        log_w = b_last - bb + ig
        max_lw = jnp.max(log_w, axis=1, keepdims=True)
        m = m_state[h]
        m_prev, m_next = [], []
        for ci in range(nc):
            m_prev.append(m)
            m = jnp.maximum(b_last[ci] + m, max_lw[ci])
            m_next.append(m)
        m_state[h] = m
        m_prev = jnp.stack(m_prev, axis=0)
        m_next = jnp.stack(m_next, axis=0)
        inter = bb + m_prev
        m_t = jnp.maximum(inter, max_ld)
        d_mat = jnp.exp(log_d - m_t[:, :, :c])
        e_inter = jnp.exp(inter - m_t)
        sc = _bdot_nt(q3, k3) * d_mat
        scv = _bdot(sc, v3)
        rs = jnp.sum(sc, axis=-1, keepdims=True)
        kw = k3 * jnp.exp(log_w - m_next)
        kvn_scr[h * nc:(h + 1) * nc] = _bdot_tn(kw, jnp.concatenate([v3, ones_v], axis=2))
        keep = jnp.exp(b_last + m_prev - m_next)
        keep_scr[h * nc:(h + 1) * nc] = jnp.concatenate([keep, keep], axis=2)
        per_head.append((e_inter, scv, rs, jnp.exp(-m_t)))

    def step(ci, carry):
        rows = pl.ds(pl.multiple_of(ci * c, c), c)
        for h in range(N_HEADS):
            hs = slice(h * hd, (h + 1) * hd)
            idx = h * nc + ci
            cn = cn_state[h]
            qcn_scr[idx] = _dot(q_ref[0, rows, hs], cn)
            cn_state[h] = keep_scr[idx] * cn + kvn_scr[idx]
        return carry

    lax.fori_loop(0, nc, step, 0)
    for h in range(N_HEADS):
        hs = slice(h * hd, (h + 1) * hd)
        e_inter, scv, rs, e_mt = per_head[h]
        qcn = qcn_scr[h * nc:(h + 1) * nc]
        num = e_inter * qcn[:, :, :hd] + scv
        den = e_inter * qcn[:, :, hd:] + rs
        h_scr[:, hs] = (num / jnp.maximum(jnp.abs(den), e_mt)).reshape(ts, hd)
    y = _head_rms(h_scr[...], gn_ref[...]) * _sigmoid(og_ref[0].astype(F32))
    o_ref[0] = y.astype(o_ref.dtype)


def mlstm_mixer(proj, gates, i_bias_v, f_bias_v, gn_d, ts=512):
    b, s, _ = proj.shape
    w = GROUP_WIDTH
    hd = HEAD_DIM
    units = N_HEADS * (ts // CHUNK)

    def col(off):
        return pl.BlockSpec((1, ts, w), lambda bi, i, off=off: (bi, i, off))

    vec = pl.BlockSpec((1, GATE_LANES), lambda bi, i: (0, 0))
    big = lambda: pltpu.VMEM((ts, w), F32)
    return pl.pallas_call(
        _mlstm_kernel,
        grid=(b, s // ts),
        in_specs=[col(COL_DQ), col(COL_DK), col(COL_DV), col(COL_DO),
                  pl.BlockSpec((1, ts, GATE_LANES), lambda bi, i: (bi, i, 0)),
                  vec, vec, pl.BlockSpec((1, w), lambda bi, i: (0, 0))],
        out_specs=pl.BlockSpec((1, ts, w), lambda bi, i: (bi, i, 0)),
        out_shape=jax.ShapeDtypeStruct((b, s, w), BF16),
        scratch_shapes=[pltpu.VMEM((units, hd, 2 * hd), F32), pltpu.VMEM((units, 1, 2 * hd), F32),
                        pltpu.VMEM((units, CHUNK, 2 * hd), F32), big(),
                        pltpu.VMEM((N_HEADS, hd, 2 * hd), F32),
                        pltpu.VMEM((N_HEADS, 1, hd), F32)],
        compiler_params=_cparams(("arbitrary", "arbitrary")),
        name="mlstm",
    )(proj, proj, proj, proj, gates, i_bias_v, f_bias_v, gn_d)


def _router_kernel(x_ref, sc_ref, sh_ref, nw_ref, wr_ref, rb_ref, h_ref, g_ref):
    h = _modulated_norm(x_ref[0], nw_ref[...], sc_ref[0], sh_ref[0])
    h_ref[0] = h.astype(BF16)
    tm = h.shape[0]
    logits = lax.dot_general(wr_ref[...], h, (((1,), (1,)), ((), ())),
                             precision=HIGHEST, preferred_element_type=F32)
    scores = _sigmoid(logits)
    biased = scores + rb_ref[...][:, 0:1]
    sr = [scores[e:e + 1, :] for e in range(N_EXPERTS)]
    br = [biased[e:e + 1, :] for e in range(N_EXPERTS)]
    n_groups = N_EXPERTS // EXPERTS_PER_GROUP
    group_scores = []
    for g in range(n_groups):
        a, b, cc, d = br[4 * g:4 * g + 4]
        p, q = jnp.maximum(a, b), jnp.minimum(a, b)
        r, s = jnp.maximum(cc, d), jnp.minimum(cc, d)
        group_scores.append(jnp.maximum(p, r) + jnp.maximum(jnp.minimum(p, r), jnp.maximum(q, s)))
    sel = jnp.zeros((1, tm), jnp.int32)
    best = group_scores[0]
    for g in range(1, n_groups):
        better = group_scores[g] > best
        sel = jnp.where(better, g, sel)
        best = jnp.where(better, group_scores[g], best)
    masked = [jnp.where(sel == (e // EXPERTS_PER_GROUP), br[e], -jnp.inf) for e in range(N_EXPERTS)]

    def top1(vals):
        idx = jnp.zeros((1, tm), jnp.int32)
        bv = vals[0]
        for e in range(1, N_EXPERTS):
            better = vals[e] > bv
            idx = jnp.where(better, e, idx)
            bv = jnp.where(better, vals[e], bv)
        return idx

    i1 = top1(masked)
    i2 = top1([jnp.where(i1 == e, -jnp.inf, masked[e]) for e in range(N_EXPERTS)])
    s1 = sum(jnp.where(i1 == e, sr[e], 0.0) for e in range(N_EXPERTS))
    s2 = sum(jnp.where(i2 == e, sr[e], 0.0) for e in range(N_EXPERTS))
    tot = s1 + s2
    g_rows = [jnp.where(i1 == e, s1 / tot, 0.0) + jnp.where(i2 == e, s2 / tot, 0.0)
              for e in range(N_EXPERTS)]
    gates_t = jnp.concatenate(g_rows + [jnp.zeros((GATE_LANES - N_EXPERTS, tm), F32)], axis=0)
    g_ref[0] = gates_t.T


def router(x, scale, shift, norm_w, w_router_t, router_bias_col, tm=512):
    b, s, d = x.shape
    return pl.pallas_call(
        _router_kernel,
        grid=(b, s // tm),
        in_specs=[pl.BlockSpec((1, tm, d), lambda bi, i: (bi, i, 0)),
                  pl.BlockSpec((1, 1, d), lambda bi, i: (bi, 0, 0)),
                  pl.BlockSpec((1, 1, d), lambda bi, i: (bi, 0, 0)),
                  pl.BlockSpec((1, d), lambda bi, i: (0, 0)),
                  pl.BlockSpec((N_EXPERTS, d), lambda bi, i: (0, 0)),
                  pl.BlockSpec((N_EXPERTS, GATE_LANES), lambda bi, i: (0, 0))],
        out_specs=[pl.BlockSpec((1, tm, d), lambda bi, i: (bi, i, 0)),
                   pl.BlockSpec((1, tm, GATE_LANES), lambda bi, i: (bi, i, 0))],
        out_shape=[jax.ShapeDtypeStruct((b, s, d), BF16),
                   jax.ShapeDtypeStruct((b, s, GATE_LANES), F32)],
        compiler_params=_cparams(("arbitrary", "arbitrary")),
        name="router",
    )(x, scale, shift, norm_w, w_router_t, router_bias_col)


def _moe_kernel(x_ref, h_ref, g_ref, wg_ref, wu_ref, wd_ref, gate_ref, o_ref, acc):
    e = pl.program_id(2)

    @pl.when(e == 0)
    def _():
        acc[...] = jnp.zeros(acc.shape, F32)

    h = h_ref[0]
    hg = jnp.dot(h, wg_ref[0], preferred_element_type=F32)
    hu = jnp.dot(h, wu_ref[0], preferred_element_type=F32)
    gates = g_ref[0]
    lane = lax.broadcasted_iota(jnp.int32, gates.shape, 1)
    gcol = jnp.sum(jnp.where(lane == e, gates, 0.0), axis=-1, keepdims=True)
    act = _silu(hg) * hu * gcol
    acc[...] += _dot(act, wd_ref[0])

    @pl.when(e == pl.num_programs(2) - 1)
    def _():
        o_ref[0] = x_ref[0] + gate_ref[0] * acc[...]


def moe_dense(x, h, gates, w_gate, w_up, w_down, gate, tm=512):
    b, s, d = x.shape
    ne, _, dff = w_gate.shape
    return pl.pallas_call(
        _moe_kernel,
        grid=(b, s // tm, ne),
        in_specs=[pl.BlockSpec((1, tm, d), lambda bi, i, e: (bi, i, 0)),
                  pl.BlockSpec((1, tm, d), lambda bi, i, e: (bi, i, 0)),
                  pl.BlockSpec((1, tm, GATE_LANES), lambda bi, i, e: (bi, i, 0)),
                  pl.BlockSpec((1, d, dff), lambda bi, i, e: (e, 0, 0)),
                  pl.BlockSpec((1, d, dff), lambda bi, i, e: (e, 0, 0)),
                  pl.BlockSpec((1, dff, d), lambda bi, i, e: (e, 0, 0)),
                  pl.BlockSpec((1, 1, d), lambda bi, i, e: (bi, 0, 0))],
        out_specs=pl.BlockSpec((1, tm, d), lambda bi, i, e: (bi, i, 0)),
        out_shape=jax.ShapeDtypeStruct((b, s, d), F32),
        scratch_shapes=[pltpu.VMEM((tm, d), F32)],
        compiler_params=_cparams(("arbitrary", "arbitrary", "arbitrary")),
        name="moe_dense",
    )(x, h, gates, w_gate, w_up, w_down, gate)


def _final_norm_kernel(x_ref, nw_ref, o_ref):
    x = x_ref[0]
    ms = jnp.mean(x * x, axis=-1, keepdims=True)
    o_ref[0] = x * lax.rsqrt(ms + EPS) * nw_ref[...]


def final_norm(x, norm_w, tm=1024):
    b, s, d = x.shape
    return pl.pallas_call(
        _final_norm_kernel,
        grid=(b, s // tm),
        in_specs=[pl.BlockSpec((1, tm, d), lambda bi, i: (bi, i, 0)),
                  pl.BlockSpec((1, d), lambda bi, i: (0, 0))],
        out_specs=pl.BlockSpec((1, tm, d), lambda bi, i: (bi, i, 0)),
        out_shape=jax.ShapeDtypeStruct((b, s, d), F32),
        compiler_params=_cparams(("arbitrary", "arbitrary")),
        name="final_norm",
    )(x, norm_w)


def _split_w_in(w_in):
    gw = GROUP_WIDTH
    a_end = 3 * gw
    b_end = a_end + 2 * gw
    c_main_end = b_end + 4 * gw
    c_end = c_main_end + 2 * N_HEADS
    d_main_end = c_end + 4 * gw
    main = jnp.concatenate([w_in[:, :c_main_end], w_in[:, c_end:d_main_end]], axis=1)
    small = jnp.concatenate([w_in[:, c_main_end:c_end], w_in[:, d_main_end:]], axis=1)
    small = jnp.pad(small, ((0, 0), (0, GATE_LANES - small.shape[1])))
    return main.astype(BF16), small.astype(BF16)


def _lanes(vec, offset):
    return jnp.zeros((1, GATE_LANES), F32).at[0, offset:offset + N_HEADS].set(vec)


def _rope_tables(s):
    half = HEAD_DIM // 2
    inv_freq = ROPE_THETA ** (-jnp.arange(half, dtype=F32) / half)
    ang = jnp.arange(s, dtype=F32)[:, None] * inv_freq[None, :]
    cos, sin = jnp.cos(ang), jnp.sin(ang)
    return jnp.concatenate([cos, cos], axis=1), jnp.concatenate([-sin, sin], axis=1)


def hybrid_mixer_layer(x, sc1, sh1, g1, norm_w, w_in, w_out, gn_a, conv_b_w, conv_b_b, w_rg, b_rg,
                       w_ig, b_ig, lru_lambda, gn_b, conv_c_w, gdn_a_log, gdn_dt_bias, gdn_norm,
                       mlstm_i_bias, mlstm_f_bias, gn_d, rope):
    w_main, w_small = _split_w_in(w_in)
    proj, gates = in_projection(x, sc1, sh1, norm_w.reshape(1, -1), w_main, w_small)
    row = lambda v: v.reshape(1, -1)
    y_a = dilated_attention_mixer(proj, rope[0], rope[1], row(gn_a))
    y_b = rg_lru_mixer(proj, conv_b_w, row(conv_b_b), w_rg.astype(BF16), row(b_rg),
                       w_ig.astype(BF16), row(b_ig), row(lru_lambda), row(gn_b))
    y_c = gated_delta_mixer(proj, gates, conv_c_w, _lanes(gdn_a_log, N_HEADS),
                            _lanes(gdn_dt_bias, N_HEADS), row(jnp.tile(gdn_norm, N_HEADS)))
    y_d = mlstm_mixer(proj, gates, _lanes(mlstm_i_bias, 2 * N_HEADS),
                      _lanes(mlstm_f_bias, 3 * N_HEADS), row(gn_d))
    return out_projection(x, (y_a, y_b, y_c, y_d), w_out.astype(BF16), g1)


def moe_layer(x, sc2, sh2, g2, norm_w, w_router, router_bias, w_gate, w_up, w_down):
    rb = jnp.broadcast_to(router_bias.reshape(-1, 1), (N_EXPERTS, GATE_LANES))
    h, gates = router(x, sc2, sh2, norm_w.reshape(1, -1), w_router.T, rb)
    return moe_dense(x, h, gates, w_gate.astype(BF16), w_up.astype(BF16), w_down.astype(BF16), g2)


def kernel(x, c, norm_mix, norm_ffn, norm_final, w_ada, b_ada, w_in, w_out, gn_a, conv_b_w, conv_b_b, w_rg, b_rg, w_ig, b_ig, lru_lambda, gn_b, conv_c_w, gdn_a_log, gdn_dt_bias, gdn_norm, mlstm_i_bias, mlstm_f_bias, gn_d, w_router, router_bias, moe_w_gate, moe_w_up, moe_w_down):
    depth = w_ada.shape[0]
    b, s, d = x.shape
    mod = ada_modulation(c, w_ada, b_ada)
    rope = _rope_tables(s)
    for l in range(depth):
        sh1, sc1, g1, sh2, sc2, g2 = [mod[l, :, None, i * d:(i + 1) * d] for i in range(6)]
        x = hybrid_mixer_layer(x, sc1, sh1, g1, norm_mix[l], w_in[l], w_out[l], gn_a[l],
                               conv_b_w[l], conv_b_b[l], w_rg[l], b_rg[l], w_ig[l], b_ig[l],
                               lru_lambda[l], gn_b[l], conv_c_w[l], gdn_a_log[l], gdn_dt_bias[l],
                               gdn_norm[l], mlstm_i_bias[l], mlstm_f_bias[l], gn_d[l], rope)
        x = moe_layer(x, sc2, sh2, g2, norm_ffn[l], w_router, router_bias,
                      moe_w_gate[l], moe_w_up[l], moe_w_down[l])
    return final_norm(x, norm_final.reshape(1, -1))
```

```python
import functools
import math

import jax
import jax.numpy as jnp
from jax import lax
from jax.experimental import pallas as pl
from jax.experimental.pallas import tpu as pltpu

F32 = jnp.float32
BF16 = jnp.bfloat16
HIGHEST = lax.Precision.HIGHEST

HEAD_DIM = 128
N_HEADS = 4
GROUP_WIDTH = N_HEADS * HEAD_DIM
DILATED_PAIRS = ((128, 1), (512, 4), (2048, 16))
ATTN_BLOCK = 128
ROPE_THETA = 10000.0
CONV_WIDTH = 4
LRU_C = 8.0
CHUNK = 64
N_EXPERTS = 16
EXPERTS_PER_GROUP = 4
EPS = 1e-6
M_INIT = -1e30
NEG = -1e30
SUBLANES = 8
VMEM_LIMIT = 56 * 1024 * 1024

COL_AQ, COL_AK, COL_AV = 0, 4, 8
COL_BX, COL_BG = 3, 4
COL_CQ, COL_CK, COL_CV, COL_CZ = 5, 6, 7, 8
COL_DQ, COL_DK, COL_DV, COL_DO = 9, 10, 11, 12
MAIN_COLS = 13 * GROUP_WIDTH
GATE_LANES = 128


def _cparams(sem):
    return pltpu.CompilerParams(dimension_semantics=sem, vmem_limit_bytes=VMEM_LIMIT)


def _dot(a, b):
    return jnp.dot(a.astype(BF16), b.astype(BF16), preferred_element_type=F32)


def _dot_nt(a, b):
    return lax.dot_general(a.astype(BF16), b.astype(BF16), (((1,), (1,)), ((), ())),
                           preferred_element_type=F32)


def _dot_tn(a, b):
    return lax.dot_general(a.astype(BF16), b.astype(BF16), (((0,), (0,)), ((), ())),
                           preferred_element_type=F32)


def _dot_exact(a, b):
    return jnp.dot(a, b, precision=HIGHEST, preferred_element_type=F32)


def _row_form(col_b):
    c = col_b.shape[0]
    lane = lax.broadcasted_iota(jnp.int32, col_b.shape, 1)
    picked = jnp.where(lane == 0, col_b, 0.0)
    return lax.dot_general(jnp.ones((c, col_b.shape[1]), F32), picked, (((1,), (1,)), ((), ())),
                           precision=HIGHEST, preferred_element_type=F32)


def _softplus(z):
    return jnp.maximum(z, 0.0) + jnp.log1p(jnp.exp(-jnp.abs(z)))


def _sigmoid(z):
    return 1.0 / (1.0 + jnp.exp(-z))


def _silu(z):
    return z * _sigmoid(z)


def _head_rms(x, gain):
    outs = []
    for g in range(x.shape[1] // HEAD_DIM):
        xg = x[:, g * HEAD_DIM:(g + 1) * HEAD_DIM]
        ms = jnp.mean(xg * xg, axis=-1, keepdims=True)
        outs.append(xg * lax.rsqrt(ms + EPS))
    y = outs[0] if len(outs) == 1 else jnp.concatenate(outs, axis=1)
    return y * gain


def _tril(c, strict=False):
    row = lax.broadcasted_iota(jnp.int32, (c, c), 0)
    col = lax.broadcasted_iota(jnp.int32, (c, c), 1)
    return (col < row) if strict else (col <= row)


def _ada_kernel(c_ref, w_ref, b_ref, o_ref):
    c = c_ref[...]
    o_ref[0] = _dot(_silu(c), w_ref[0]) + b_ref[0]


def ada_modulation(c, w_ada, b_ada):
    depth, d, n = w_ada.shape
    b = c.shape[0]
    rows = -(-b // SUBLANES) * SUBLANES
    c_pad = jnp.zeros((rows, d), F32).at[:b].set(c)
    tn = 1024
    out = pl.pallas_call(
        _ada_kernel,
        grid=(depth, n // tn),
        in_specs=[pl.BlockSpec((rows, d), lambda l, j: (0, 0)),
                  pl.BlockSpec((1, d, tn), lambda l, j: (l, 0, j)),
                  pl.BlockSpec((1, 1, tn), lambda l, j: (l, 0, j))],
        out_specs=pl.BlockSpec((1, rows, tn), lambda l, j: (l, 0, j)),
        out_shape=jax.ShapeDtypeStruct((depth, rows, n), F32),
        compiler_params=_cparams(("arbitrary", "arbitrary")),
        name="ada_modulation",
    )(c_pad, w_ada, b_ada.reshape(depth, 1, n))
    return out[:, :b]


def _modulated_norm(x, nw, sc, sh):
    ms = jnp.mean(x * x, axis=-1, keepdims=True)
    return (x * lax.rsqrt(ms + EPS) * nw) * (1.0 + sc) + sh


def _in_proj_kernel(x_ref, sc_ref, sh_ref, nw_ref, w_ref, wg_ref, proj_ref, gates_ref, h_scr):
    @pl.when(pl.program_id(2) == 0)
    def _():
        h = _modulated_norm(x_ref[0], nw_ref[...], sc_ref[0], sh_ref[0]).astype(BF16)
        h_scr[...] = h
        gates_ref[0] = jnp.dot(h, wg_ref[...], preferred_element_type=F32)

    proj_ref[0] = jnp.dot(h_scr[...], w_ref[...], preferred_element_type=F32).astype(BF16)


def in_projection(x, scale, shift, norm_w, w_main, w_gate, tm=512, tn=1664):
    b, s, d = x.shape
    n = w_main.shape[1]
    return pl.pallas_call(
        _in_proj_kernel,
        grid=(b, s // tm, n // tn),
        in_specs=[pl.BlockSpec((1, tm, d), lambda bi, i, j: (bi, i, 0)),
                  pl.BlockSpec((1, 1, d), lambda bi, i, j: (bi, 0, 0)),
                  pl.BlockSpec((1, 1, d), lambda bi, i, j: (bi, 0, 0)),
                  pl.BlockSpec((1, d), lambda bi, i, j: (0, 0)),
                  pl.BlockSpec((d, tn), lambda bi, i, j: (0, j)),
                  pl.BlockSpec((d, GATE_LANES), lambda bi, i, j: (0, 0))],
        out_specs=[pl.BlockSpec((1, tm, tn), lambda bi, i, j: (bi, i, j)),
                   pl.BlockSpec((1, tm, GATE_LANES), lambda bi, i, j: (bi, i, 0))],
        out_shape=[jax.ShapeDtypeStruct((b, s, n), BF16),
                   jax.ShapeDtypeStruct((b, s, GATE_LANES), F32)],
        scratch_shapes=[pltpu.VMEM((tm, d), BF16)],
        compiler_params=_cparams(("arbitrary", "arbitrary", "arbitrary")),
        name="in_projection",
    )(x, scale, shift, norm_w, w_main, w_gate)


def _out_proj_kernel(x_ref, ya_ref, yb_ref, yc_ref, yd_ref, w_ref, g_ref, o_ref):
    acc = None
    for k, y_ref in enumerate((ya_ref, yb_ref, yc_ref, yd_ref)):
        part = jnp.dot(y_ref[0], w_ref[k * GROUP_WIDTH:(k + 1) * GROUP_WIDTH, :],
                       preferred_element_type=F32)
        acc = part if acc is None else acc + part
    o_ref[0] = x_ref[0] + g_ref[0] * acc


def out_projection(x, ys, w_out, gate, tm=512):
    b, s, d = x.shape
    yspec = pl.BlockSpec((1, tm, GROUP_WIDTH), lambda bi, i: (bi, i, 0))
    return pl.pallas_call(
        _out_proj_kernel,
        grid=(b, s // tm),
        in_specs=[pl.BlockSpec((1, tm, d), lambda bi, i: (bi, i, 0)),
                  yspec, yspec, yspec, yspec,
                  pl.BlockSpec(w_out.shape, lambda bi, i: (0, 0)),
                  pl.BlockSpec((1, 1, d), lambda bi, i: (bi, 0, 0))],
        out_specs=pl.BlockSpec((1, tm, d), lambda bi, i: (bi, i, 0)),
        out_shape=jax.ShapeDtypeStruct((b, s, d), F32),
        compiler_params=_cparams(("arbitrary", "arbitrary")),
        name="out_projection",
    )(x, *ys, w_out, gate)


def _attn_kernel(q_ref, k_ref, v_ref, cos_ref, sin_ref, gn_ref, o_ref, qs, ks, vs, acc, ms, ls):
    s = q_ref.shape[1]
    blk = ATTN_BLOCK
    cos = cos_ref[...]
    sin = sin_ref[...]

    def rot(t):
        return t * cos + pltpu.roll(t, HEAD_DIM // 2, 1) * sin

    qs[...] = rot(q_ref[0].astype(F32)) * (1.0 / math.sqrt(HEAD_DIM))
    ks[...] = rot(k_ref[0].astype(F32))
    vs[...] = v_ref[0].astype(F32)

    row = lax.broadcasted_iota(jnp.int32, (blk, 2 * blk), 0)
    col = lax.broadcasted_iota(jnp.int32, (blk, 2 * blk), 1)
    own_ok = (col >= blk) & (col - blk <= row)

    for branch, (window, dil) in enumerate(DILATED_PAIRS):
        nb = s // (blk * dil)

        def rows(start, dil=dil):
            if dil == 1:
                return pl.ds(pl.multiple_of(start, blk), blk)
            return pl.ds(start, blk, stride=dil)

        def body(idx, carry, dil=dil, nb=nb, branch=branch, rows=rows):
            r = idx // nb
            n = idx % nb
            start = r + n * (blk * dil)
            pstart = jnp.where(n > 0, start - blk * dil, start)
            qb = qs[rows(start), :]
            kc = jnp.concatenate([ks[rows(pstart), :], ks[rows(start), :]], axis=0)
            vc = jnp.concatenate([vs[rows(pstart), :], vs[rows(start), :]], axis=0)
            sc = _dot_nt(qb, kc)
            first = jnp.where(n > 0, 0, 4 * blk)
            valid = own_ok | ((col < blk) & (col >= row + first))
            sc = jnp.where(valid, sc, NEG)
            mb = jnp.max(sc, axis=-1, keepdims=True)
            p = jnp.exp(sc - mb)
            den = jnp.sum(p, axis=-1, keepdims=True)
            num = _dot(p, vc)
            mb = jnp.broadcast_to(mb, (blk, HEAD_DIM))
            den = jnp.broadcast_to(den, (blk, HEAD_DIM))
            if branch == 0:
                acc[rows(start), :] = num
                ms[rows(start), :] = mb
                ls[rows(start), :] = den
            else:
                m_old = ms[rows(start), :]
                m_new = jnp.maximum(m_old, mb)
                a_old = jnp.exp(m_old - m_new)
                a_blk = jnp.exp(mb - m_new)
                acc[rows(start), :] = acc[rows(start), :] * a_old + num * a_blk
                ls[rows(start), :] = ls[rows(start), :] * a_old + den * a_blk
                ms[rows(start), :] = m_new
            return carry

        lax.fori_loop(0, s // blk, body, 0, unroll=8)

    out = acc[...] / ls[...]
    o_ref[0] = _head_rms(out, gn_ref[...]).astype(o_ref.dtype)


def dilated_attention_mixer(proj, cos2, sin2, gn_a):
    b, s, _ = proj.shape
    hd = HEAD_DIM

    def col(off):
        return pl.BlockSpec((1, s, hd), lambda bi, h, off=off: (bi, 0, off + h))

    return pl.pallas_call(
        _attn_kernel,
        grid=(b, N_HEADS),
        in_specs=[col(COL_AQ), col(COL_AK), col(COL_AV),
                  pl.BlockSpec((s, hd), lambda bi, h: (0, 0)),
                  pl.BlockSpec((s, hd), lambda bi, h: (0, 0)),
                  pl.BlockSpec((1, hd), lambda bi, h: (0, h))],
        out_specs=pl.BlockSpec((1, s, hd), lambda bi, h: (bi, 0, h)),
        out_shape=jax.ShapeDtypeStruct((b, s, GROUP_WIDTH), BF16),
        scratch_shapes=[pltpu.VMEM((s, hd), F32) for _ in range(6)],
        compiler_params=_cparams(("arbitrary", "arbitrary")),
        name="dilated_attention",
    )(proj, proj, proj, cos2, sin2, gn_a)


def _load_conv_tile(x_ref, xpad, ts):
    @pl.when(pl.program_id(1) == 0)
    def _():
        xpad[0:SUBLANES, :] = jnp.zeros((SUBLANES, xpad.shape[1]), F32)

    @pl.when(pl.program_id(1) > 0)
    def _():
        xpad[0:SUBLANES, :] = xpad[ts:ts + SUBLANES, :]

    xpad[SUBLANES:SUBLANES + ts, :] = x_ref[0].astype(F32)


def _conv_from_pad(xpad, w, ts):
    y = None
    for j in range(CONV_WIDTH):
        off = SUBLANES - (CONV_WIDTH - 1) + j
        term = xpad[off:off + ts, :] * w[j:j + 1, :]
        y = term if y is None else y + term
    return y


def _lru_kernel(x_ref, g_ref, cw_ref, cb_ref, wr_ref, br_ref, wi_ref, bi_ref, lam_ref, gn_ref,
                o_ref, xpad, a_scr, u_scr, carry):
    ts = x_ref.shape[1]
    w = x_ref.shape[2]
    _load_conv_tile(x_ref, xpad, ts)

    @pl.when(pl.program_id(1) == 0)
    def _():
        carry[...] = jnp.zeros(carry.shape, F32)

    xc = _conv_from_pad(xpad, cw_ref[...], ts) + cb_ref[...]
    r_parts, i_parts = [], []
    for g in range(N_HEADS):
        xg = xc[:, g * HEAD_DIM:(g + 1) * HEAD_DIM]
        r_parts.append(_dot(xg, wr_ref[g]))
        i_parts.append(_dot(xg, wi_ref[g]))
    r = _sigmoid(jnp.concatenate(r_parts, axis=1) + br_ref[...])
    ig = _sigmoid(jnp.concatenate(i_parts, axis=1) + bi_ref[...])
    log_a = (-LRU_C) * r * _softplus(-lam_ref[...])
    a_scr[...] = jnp.exp(log_a)
    u_scr[...] = jnp.sqrt(1.0 - jnp.exp(2.0 * log_a)) * (ig * xc)

    sub = lax.broadcasted_iota(jnp.int32, (SUBLANES, w), 0)

    def body(j, h_prev):
        rows = pl.ds(pl.multiple_of(j * SUBLANES, SUBLANES), SUBLANES)
        a = a_scr[rows, :]
        u = u_scr[rows, :]
        for sh in (1, 2, 4):
            keep = sub >= sh
            a_sh = jnp.where(keep, pltpu.roll(a, sh, 0), 1.0)
            u_sh = jnp.where(keep, pltpu.roll(u, sh, 0), 0.0)
            u = a * u_sh + u
            a = a * a_sh
        h = u + a * h_prev
        u_scr[rows, :] = h
        return jnp.broadcast_to(h[SUBLANES - 1:SUBLANES, :], (SUBLANES, w))

    carry[...] = lax.fori_loop(0, ts // SUBLANES, body, carry[...])
    y = _head_rms(u_scr[...], gn_ref[...]) * jax.nn.gelu(g_ref[0].astype(F32))
    o_ref[0] = y.astype(o_ref.dtype)


def rg_lru_mixer(proj, conv_w, conv_b, w_rg, b_rg, w_ig, b_ig, lam, gn_b, ts=1024):
    b, s, _ = proj.shape
    w = GROUP_WIDTH
    vec = pl.BlockSpec((1, w), lambda bi, i: (0, 0))
    blockdiag = pl.BlockSpec((N_HEADS, HEAD_DIM, HEAD_DIM), lambda bi, i: (0, 0, 0))
    return pl.pallas_call(
        _lru_kernel,
        grid=(b, s // ts),
        in_specs=[pl.BlockSpec((1, ts, w), lambda bi, i: (bi, i, COL_BX)),
                  pl.BlockSpec((1, ts, w), lambda bi, i: (bi, i, COL_BG)),
                  pl.BlockSpec((CONV_WIDTH, w), lambda bi, i: (0, 0)),
                  vec, blockdiag, vec, blockdiag, vec, vec, vec],
        out_specs=pl.BlockSpec((1, ts, w), lambda bi, i: (bi, i, 0)),
        out_shape=jax.ShapeDtypeStruct((b, s, w), BF16),
        scratch_shapes=[pltpu.VMEM((ts + SUBLANES, w), F32), pltpu.VMEM((ts, w), F32),
                        pltpu.VMEM((ts, w), F32), pltpu.VMEM((SUBLANES, w), F32)],
        compiler_params=_cparams(("arbitrary", "arbitrary")),
        name="rg_lru",
    )(proj, proj, conv_w, conv_b, w_rg, b_rg, w_ig, b_ig, lam, gn_b)


def _lane_pick(x, lane_idx):
    lane = lax.broadcasted_iota(jnp.int32, x.shape, 1)
    col = jnp.sum(jnp.where(lane == lane_idx, x, 0.0), axis=-1, keepdims=True)
    return jnp.broadcast_to(col, x.shape)


def _bdot(a, b):
    return jnp.einsum("nij,njk->nik", a.astype(BF16), b.astype(BF16), preferred_element_type=F32)


def _bdot_nt(a, b):
    return jnp.einsum("nid,njd->nij", a.astype(BF16), b.astype(BF16), preferred_element_type=F32)


def _bdot_tn(a, b):
    return jnp.einsum("ncd,nce->nde", a.astype(BF16), b.astype(BF16), preferred_element_type=F32)


def _chunk_cumsum(x, c):
    nc = x.shape[0] // c
    tri = jnp.broadcast_to(_tril(c).astype(F32), (nc, c, c))
    return jnp.einsum("nij,njk->nik", tri, x.reshape(nc, c, x.shape[1]), precision=HIGHEST,
                      preferred_element_type=F32)


def _row_forms(x3, specs):
    nc, c, w = x3.shape
    rows = len(specs) * c
    lane = lax.broadcasted_iota(jnp.int32, (rows, w), 1)
    row = lax.broadcasted_iota(jnp.int32, (rows, w), 0)
    sel = jnp.zeros((rows, w), F32)
    for i, spec in enumerate(specs):
        in_rows = (row >= i * c) & (row < (i + 1) * c)
        for ln, coeff in spec:
            sel = jnp.where(in_rows & (lane == ln), coeff, sel)
    out = jnp.einsum("nil,njl->nij", jnp.broadcast_to(sel, (nc, rows, w)), x3, precision=HIGHEST,
                     preferred_element_type=F32)
    return [out[:, i * c:(i + 1) * c, :] for i in range(len(specs))]


def _lane_pick3(x3, lane_idx):
    lane = lax.broadcasted_iota(jnp.int32, x3.shape, 2)
    col = jnp.sum(jnp.where(lane == lane_idx, x3, 0.0), axis=-1, keepdims=True)
    return jnp.broadcast_to(col, x3.shape[:2] + (HEAD_DIM,))


def _gdn_kernel(q_ref, k_ref, v_ref, z_ref, gt_ref, cwq_ref, cwk_ref, cwv_ref, alog_ref, dtb_ref,
                gn_ref, o_ref, qpad, kpad, vpad, p_scr, n_scr, qp_scr, op_scr, gl_scr, o_scr, state):
    ts = q_ref.shape[1]
    c = CHUNK
    hd = HEAD_DIM
    nc = ts // c
    _load_conv_tile(q_ref, qpad, ts)
    _load_conv_tile(k_ref, kpad, ts)
    _load_conv_tile(v_ref, vpad, ts)

    @pl.when(pl.program_id(1) == 0)
    def _():
        state[...] = jnp.zeros(state.shape, F32)

    qc = _silu(_conv_from_pad(qpad, cwq_ref[...], ts))
    kc = _silu(_conv_from_pad(kpad, cwk_ref[...], ts))
    vc = _silu(_conv_from_pad(vpad, cwv_ref[...], ts))
    gates = gt_ref[0]
    neg_a = -jnp.exp(alog_ref[...])
    g_all = neg_a * _softplus(gates + dtb_ref[...])
    beta3_all = _sigmoid(gates).reshape(nc, c, GATE_LANES)
    gc3_all = _chunk_cumsum(g_all, c)
    gc_rows = _row_forms(gc3_all, [[(N_HEADS + h, 1.0)] for h in range(N_HEADS)])
    incl = _tril(c)
    strict = _tril(c, strict=True)
    eye = (lax.broadcasted_iota(jnp.int32, (c, c), 0)
           == lax.broadcasted_iota(jnp.int32, (c, c), 1)).astype(F32)

    q_u, k_u, kb_u, vb_u, kbe_u, qd_u, kd_u, dec_u, gl_u = ([] for _ in range(9))
    for h in range(N_HEADS):
        hs = slice(h * hd, (h + 1) * hd)
        qh = qc[:, hs]
        kh = kc[:, hs]
        qn = (qh * lax.rsqrt(jnp.sum(qh * qh, axis=-1, keepdims=True) + EPS)
              * (hd ** -0.5)).reshape(nc, c, hd)
        kn = (kh * lax.rsqrt(jnp.sum(kh * kh, axis=-1, keepdims=True) + EPS)).reshape(nc, c, hd)
        beta = _lane_pick3(beta3_all, h)
        gc = _lane_pick3(gc3_all, N_HEADS + h)
        e_gc = jnp.exp(gc)
        gc_last = gc[:, c - 1:c, :]
        kb = kn * beta
        q_u.append(qn.astype(BF16))
        k_u.append(kn.astype(BF16))
        kb_u.append(kb.astype(BF16))
        vb_u.append((vc[:, hs].reshape(nc, c, hd) * beta).astype(BF16))
        kbe_u.append((kb * e_gc).astype(BF16))
        qd_u.append(qn * e_gc)
        kd_u.append((kn * jnp.exp(gc_last - gc)).astype(BF16))
        dec_u.append(jnp.exp(jnp.where(incl, gc[:, :, :c] - gc_rows[h], NEG)))
        gl_u.append(jnp.exp(gc_last))
    cat = lambda parts: jnp.concatenate(parts, axis=0)
    q_u, k_u, kb_u, vb_u, kbe_u, qd_u, kd_u, dec_u = (
        cat(t) for t in (q_u, k_u, kb_u, vb_u, kbe_u, qd_u, kd_u, dec_u))
    gl_scr[...] = cat(gl_u)

    a_mat = jnp.where(strict, _bdot_nt(kb_u, k_u) * dec_u, 0.0)
    t_inv = eye - a_mat
    a_pow = a_mat
    for _ in range(int(math.log2(c)) - 1):
        a_pow = _bdot(a_pow, a_pow)
        t_inv = t_inv + _bdot(t_inv, a_pow)
    u = _bdot(t_inv, vb_u)
    w = _bdot(t_inv, kbe_u)
    qk = jnp.where(incl, _bdot_nt(q_u, k_u) * dec_u, 0.0)
    p_scr[...] = _bdot_tn(kd_u, w).astype(BF16)
    n_scr[...] = _bdot_tn(kd_u, u)
    qp_scr[...] = (qd_u - _bdot(qk, w)).astype(BF16)
    op_scr[...] = _bdot(qk, u)

    def step(ci, carry):
        rows = pl.ds(pl.multiple_of(ci * c, c), c)
        for h in range(N_HEADS):
            hs = slice(h * hd, (h + 1) * hd)
            idx = h * nc + ci
            st = state[h]
            stb = st.astype(BF16)
            o_scr[rows, hs] = jnp.dot(qp_scr[idx], stb, preferred_element_type=F32) + op_scr[idx]
            state[h] = (gl_scr[idx] * st - jnp.dot(p_scr[idx], stb, preferred_element_type=F32)
                        + n_scr[idx])
        return carry

    lax.fori_loop(0, nc, step, 0)
    y = _head_rms(o_scr[...], gn_ref[...]) * _silu(z_ref[0].astype(F32))
    o_ref[0] = y.astype(o_ref.dtype)


def gated_delta_mixer(proj, gates, conv_w, a_log_v, dt_bias_v, gn_c, ts=512):
    b, s, _ = proj.shape
    w = GROUP_WIDTH
    hd = HEAD_DIM
    units = N_HEADS * (ts // CHUNK)

    def col(off):
        return pl.BlockSpec((1, ts, w), lambda bi, i, off=off: (bi, i, off))

    def cw(off):
        return pl.BlockSpec((CONV_WIDTH, w), lambda bi, i, off=off: (0, off))

    vec = pl.BlockSpec((1, GATE_LANES), lambda bi, i: (0, 0))
    big = lambda: pltpu.VMEM((ts, w), F32)
    return pl.pallas_call(
        _gdn_kernel,
        grid=(b, s // ts),
        in_specs=[col(COL_CQ), col(COL_CK), col(COL_CV), col(COL_CZ),
                  pl.BlockSpec((1, ts, GATE_LANES), lambda bi, i: (bi, i, 0)),
                  cw(0), cw(1), cw(2), vec, vec,
                  pl.BlockSpec((1, w), lambda bi, i: (0, 0))],
        out_specs=pl.BlockSpec((1, ts, w), lambda bi, i: (bi, i, 0)),
        out_shape=jax.ShapeDtypeStruct((b, s, w), BF16),
        scratch_shapes=[pltpu.VMEM((ts + SUBLANES, w), F32) for _ in range(3)]
        + [pltpu.VMEM((units, hd, hd), BF16), pltpu.VMEM((units, hd, hd), F32),
           pltpu.VMEM((units, CHUNK, hd), BF16), pltpu.VMEM((units, CHUNK, hd), F32),
           pltpu.VMEM((units, 1, hd), F32), big(),
           pltpu.VMEM((N_HEADS, hd, hd), F32)],
        compiler_params=_cparams(("arbitrary", "arbitrary")),
        name="gated_delta_net",
    )(proj, proj, proj, proj, gates, conv_w, conv_w, conv_w, a_log_v, dt_bias_v, gn_c)


def _mlstm_kernel(q_ref, k_ref, v_ref, og_ref, gt_ref, ib_ref, fb_ref, gn_ref, o_ref,
                  kvn_scr, keep_scr, qcn_scr, h_scr, cn_state, m_state):
    ts = q_ref.shape[1]
    c = CHUNK
    hd = HEAD_DIM
    nc = ts // c

    @pl.when(pl.program_id(1) == 0)
    def _():
        cn_state[...] = jnp.zeros(cn_state.shape, F32)
        m_state[...] = jnp.full(m_state.shape, M_INIT, F32)

    gates = gt_ref[0]
    i_all = gates + ib_ref[...]
    f_pre = gates + fb_ref[...]
    lf_all = jnp.minimum(f_pre, 0.0) - jnp.log1p(jnp.exp(-jnp.abs(f_pre)))
    b3_all = _chunk_cumsum(lf_all, c)
    lane = lax.broadcasted_iota(jnp.int32, (nc, c, GATE_LANES), 2)
    ib3_all = jnp.where(lane < 3 * N_HEADS, i_all.reshape(nc, c, GATE_LANES), b3_all)
    ib_rows = _row_forms(ib3_all, [[(2 * N_HEADS + h, 1.0), (3 * N_HEADS + h, -1.0)]
                                   for h in range(N_HEADS)])
    incl = _tril(c)
    k_scale = 1.0 / math.sqrt(hd)
    ones_v = jnp.ones((nc, c, hd), BF16)

    per_head = []
    for h in range(N_HEADS):
        hs = slice(h * hd, (h + 1) * hd)
        q3 = q_ref[0, :, hs].reshape(nc, c, hd)
        k3 = k_ref[0, :, hs].astype(F32).reshape(nc, c, hd) * k_scale
        v3 = v_ref[0, :, hs].reshape(nc, c, hd)
        bb = _lane_pick3(b3_all, 3 * N_HEADS + h)
        ig = _lane_pick3(ib3_all, 2 * N_HEADS + h)
        log_d = jnp.where(incl, bb[:, :, :c] + ib_rows[h], NEG)
        max_ld = jnp.max(log_d, axis=-1, keepdims=True)
        b_last = bb[:, c - 1:c, :]
        log_w = b_last - bb + ig
        max_lw = jnp.max(log_w, axis=1, keepdims=True)
        m = m_state[h]
        m_prev, m_next = [], []
        for ci in range(nc):
            m_prev.append(m)
            m = jnp.maximum(b_last[ci] + m, max_lw[ci])
            m_next.append(m)
        m_state[h] = m
        m_prev = jnp.stack(m_prev, axis=0)
        m_next = jnp.stack(m_next, axis=0)
        inter = bb + m_prev
        m_t = jnp.maximum(inter, max_ld)
        d_mat = jnp.exp(log_d - m_t[:, :, :c])
        e_inter = jnp.exp(inter - m_t)
        sc = _bdot_nt(q3, k3) * d_mat
        scv = _bdot(sc, v3)
        rs = jnp.sum(sc, axis=-1, keepdims=True)
        kw = k3 * jnp.exp(log_w - m_next)
        kvn_scr[h * nc:(h + 1) * nc] = _bdot_tn(kw, jnp.concatenate([v3, ones_v], axis=2))
        keep = jnp.exp(b_last + m_prev - m_next)
        keep_scr[h * nc:(h + 1) * nc] = jnp.concatenate([keep, keep], axis=2)
        per_head.append((e_inter, scv, rs, jnp.exp(-m_t)))

    def step(ci, carry):
        rows = pl.ds(pl.multiple_of(ci * c, c), c)
        for h in range(N_HEADS):
            hs = slice(h * hd, (h + 1) * hd)
            idx = h * nc + ci
            cn = cn_state[h]
            qcn_scr[idx] = _dot(q_ref[0, rows, hs], cn)
            cn_state[h] = keep_scr[idx] * cn + kvn_scr[idx]
        return carry

    lax.fori_loop(0, nc, step, 0)
    for h in range(N_HEADS):
        hs = slice(h * hd, (h + 1) * hd)
        e_inter, scv, rs, e_mt = per_head[h]
        qcn = qcn_scr[h * nc:(h + 1) * nc]
        num = e_inter * qcn[:, :, :hd] + scv
        den = e_inter * qcn[:, :, hd:] + rs
        h_scr[:, hs] = (num / jnp.maximum(jnp.abs(den), e_mt)).reshape(ts, hd)
    y = _head_rms(h_scr[...], gn_ref[...]) * _sigmoid(og_ref[0].astype(F32))
    o_ref[0] = y.astype(o_ref.dtype)


def mlstm_mixer(proj, gates, i_bias_v, f_bias_v, gn_d, ts=512):
    b, s, _ = proj.shape
    w = GROUP_WIDTH
    hd = HEAD_DIM
    units = N_HEADS * (ts // CHUNK)

    def col(off):
        return pl.BlockSpec((1, ts, w), lambda bi, i, off=off: (bi, i, off))

    vec = pl.BlockSpec((1, GATE_LANES), lambda bi, i: (0, 0))
    big = lambda: pltpu.VMEM((ts, w), F32)
    return pl.pallas_call(
        _mlstm_kernel,
        grid=(b, s // ts),
        in_specs=[col(COL_DQ), col(COL_DK), col(COL_DV), col(COL_DO),
                  pl.BlockSpec((1, ts, GATE_LANES), lambda bi, i: (bi, i, 0)),
                  vec, vec, pl.BlockSpec((1, w), lambda bi, i: (0, 0))],
        out_specs=pl.BlockSpec((1, ts, w), lambda bi, i: (bi, i, 0)),
        out_shape=jax.ShapeDtypeStruct((b, s, w), BF16),
        scratch_shapes=[pltpu.VMEM((units, hd, 2 * hd), F32), pltpu.VMEM((units, 1, 2 * hd), F32),
                        pltpu.VMEM((units, CHUNK, 2 * hd), F32), big(),
                        pltpu.VMEM((N_HEADS, hd, 2 * hd), F32),
                        pltpu.VMEM((N_HEADS, 1, hd), F32)],
        compiler_params=_cparams(("arbitrary", "arbitrary")),
        name="mlstm",
    )(proj, proj, proj, proj, gates, i_bias_v, f_bias_v, gn_d)


SLAB = 16
MOE_TM = 256
META_ROWS = 8


def _route(h, wr, rb_col):
    tm = h.shape[0]
    logits = lax.dot_general(wr, h, (((1,), (1,)), ((), ())),
                             precision=HIGHEST, preferred_element_type=F32)
    scores = _sigmoid(logits)
    biased = scores + rb_col
    sr = [scores[e:e + 1, :] for e in range(N_EXPERTS)]
    br = [biased[e:e + 1, :] for e in range(N_EXPERTS)]
    n_groups = N_EXPERTS // EXPERTS_PER_GROUP
    group_scores = []
    for g in range(n_groups):
        a, b, cc, d = br[4 * g:4 * g + 4]
        p, q = jnp.maximum(a, b), jnp.minimum(a, b)
        r, s = jnp.maximum(cc, d), jnp.minimum(cc, d)
        group_scores.append(jnp.maximum(p, r) + jnp.maximum(jnp.minimum(p, r), jnp.maximum(q, s)))
    sel = jnp.zeros((1, tm), jnp.int32)
    best = group_scores[0]
    for g in range(1, n_groups):
        better = group_scores[g] > best
        sel = jnp.where(better, g, sel)
        best = jnp.where(better, group_scores[g], best)
    masked = [jnp.where(sel == (e // EXPERTS_PER_GROUP), br[e], -jnp.inf) for e in range(N_EXPERTS)]

    def top1(vals):
        idx = jnp.zeros((1, tm), jnp.int32)
        bv = vals[0]
        for e in range(1, N_EXPERTS):
            better = vals[e] > bv
            idx = jnp.where(better, e, idx)
            bv = jnp.where(better, vals[e], bv)
        return idx

    i1 = top1(masked)
    i2 = top1([jnp.where(i1 == e, -jnp.inf, masked[e]) for e in range(N_EXPERTS)])
    s1 = sum(jnp.where(i1 == e, sr[e], 0.0) for e in range(N_EXPERTS))
    s2 = sum(jnp.where(i2 == e, sr[e], 0.0) for e in range(N_EXPERTS))
    tot = s1 + s2
    return i1, i2, s1 / tot, s2 / tot


def _router_kernel(x_ref, sc_ref, sh_ref, nw_ref, wr_ref, rb_ref, hs_ref, meta_ref, wts_ref, cnt_ref,
                   run):
    @pl.when((pl.program_id(0) == 0) & (pl.program_id(1) == 0))
    def _():
        run[...] = jnp.zeros(run.shape, F32)

    h = _modulated_norm(x_ref[0], nw_ref[...], sc_ref[0], sh_ref[0])
    tm = h.shape[0]
    for s in range(SLAB):
        hs_ref[pl.ds(s, tm, stride=SLAB), :] = h[:, s * HEAD_DIM:(s + 1) * HEAD_DIM]
    i1, i2, w1, w2 = _route(h, wr_ref[...], rb_ref[...][:, 0:1])
    oh1 = jnp.concatenate([(i1 == e).astype(F32) for e in range(N_EXPERTS)], axis=0)
    oh2 = jnp.concatenate([(i2 == e).astype(F32) for e in range(N_EXPERTS)], axis=0)
    before = (lax.broadcasted_iota(jnp.int32, (tm, tm), 0)
              < lax.broadcasted_iota(jnp.int32, (tm, tm), 1)).astype(BF16)
    excl1 = jnp.dot(oh1.astype(BF16), before, preferred_element_type=F32)
    excl2 = jnp.dot(oh2.astype(BF16), before, preferred_element_type=F32)
    tot1 = jnp.sum(oh1, axis=1, keepdims=True)
    tot2 = jnp.sum(oh2, axis=1, keepdims=True)
    base = run[...][:, 0:1]
    rank1 = jnp.sum(oh1 * (base + excl1), axis=0, keepdims=True)
    rank2 = jnp.sum(oh2 * (base + tot1 + excl2), axis=0, keepdims=True)
    run[...] = run[...] + (tot1 + tot2)
    cnt_ref[...] = run[...]
    meta_ref[...] = jnp.concatenate(
        [i1, i2, rank1.astype(jnp.int32), rank2.astype(jnp.int32),
         jnp.zeros((META_ROWS - 4, tm), jnp.int32)], axis=0)
    wts_ref[...] = jnp.concatenate([w1, w2, jnp.zeros((META_ROWS - 2, tm), F32)], axis=0)


def router(x, scale, shift, norm_w, w_router_t, router_bias_col, tm=512):
    b, s, d = x.shape
    nt = s // tm
    t = b * s
    return pl.pallas_call(
        _router_kernel,
        grid=(b, nt),
        in_specs=[pl.BlockSpec((1, tm, d), lambda bi, i: (bi, i, 0)),
                  pl.BlockSpec((1, 1, d), lambda bi, i: (bi, 0, 0)),
                  pl.BlockSpec((1, 1, d), lambda bi, i: (bi, 0, 0)),
                  pl.BlockSpec((1, d), lambda bi, i: (0, 0)),
                  pl.BlockSpec((N_EXPERTS, d), lambda bi, i: (0, 0)),
                  pl.BlockSpec((N_EXPERTS, GATE_LANES), lambda bi, i: (0, 0))],
        out_specs=[pl.BlockSpec((tm * SLAB, HEAD_DIM), lambda bi, i: (bi * nt + i, 0)),
                   pl.BlockSpec((META_ROWS, tm), lambda bi, i: (0, bi * nt + i)),
                   pl.BlockSpec((META_ROWS, tm), lambda bi, i: (0, bi * nt + i)),
                   pl.BlockSpec((N_EXPERTS, GATE_LANES), lambda bi, i: (0, 0))],
        out_shape=[jax.ShapeDtypeStruct((t * SLAB, HEAD_DIM), F32),
                   jax.ShapeDtypeStruct((META_ROWS, t), jnp.int32),
                   jax.ShapeDtypeStruct((META_ROWS, t), F32),
                   jax.ShapeDtypeStruct((N_EXPERTS, GATE_LANES), F32)],
        scratch_shapes=[pltpu.VMEM((N_EXPERTS, GATE_LANES), F32)],
        compiler_params=_cparams(("arbitrary", "arbitrary")),
        name="router",
    )(x, scale, shift, norm_w, w_router_t, router_bias_col)


def routing_tables(meta, counts, tm=MOE_TM):
    t = meta.shape[1]
    cnt = counts[:, 0].astype(jnp.int32)
    padded = (cnt + tm - 1) // tm * tm
    ends = jnp.cumsum(padded)
    off = ends - padded
    pos = off[meta[0:2]] + meta[2:4]
    n_tiles = 2 * t // tm + N_EXPERTS
    tile_expert = jnp.minimum(
        jnp.searchsorted(ends, jnp.arange(n_tiles, dtype=jnp.int32) * tm, side="right"),
        N_EXPERTS - 1).astype(jnp.int32)
    n_valid = (ends[-1] // tm).astype(jnp.int32).reshape(1)
    rows = n_tiles * tm
    pair = jnp.full((rows,), -1, jnp.int32).at[pos.reshape(-1)].set(jnp.arange(2 * t, dtype=jnp.int32))
    spare = jnp.minimum(jnp.cumsum((pair < 0).astype(jnp.int32)) - 1, N_EXPERTS * tm - 1)
    dest = jnp.where(pair >= 0, pair, 2 * t + spare).astype(jnp.int32)
    return pos.reshape(-1).astype(jnp.int32), tile_expert, n_valid, dest


def _slab_rows(i):
    return pl.ds(pl.multiple_of(i * SLAB, SLAB), SLAB)


def _dispatch_kernel(pos_ref, hs_ref, xz_ref, xs_ref, sem):
    del xz_ref
    tmd = hs_ref.shape[0] // SLAB
    t = pos_ref.shape[0] // 2
    base = pl.program_id(0) * tmd

    def issue(r, carry):
        for k in range(2):
            p = pos_ref[k * t + base + r]
            pltpu.make_async_copy(hs_ref.at[_slab_rows(r), :], xs_ref.at[_slab_rows(p), :], sem).start()
        return carry

    lax.fori_loop(0, tmd, issue, 0, unroll=8)
    for k in range(2):
        pltpu.make_async_copy(hs_ref, xs_ref.at[pl.ds(0, tmd * SLAB), :], sem).wait()


def dispatch(pos, hs, xs_init, tmd=512):
    t = hs.shape[0] // SLAB
    return pl.pallas_call(
        _dispatch_kernel,
        grid_spec=pltpu.PrefetchScalarGridSpec(
            num_scalar_prefetch=1,
            grid=(t // tmd,),
            in_specs=[pl.BlockSpec((tmd * SLAB, HEAD_DIM), lambda i, pos: (i, 0)),
                      pl.BlockSpec(memory_space=pl.ANY)],
            out_specs=pl.BlockSpec(memory_space=pl.ANY),
            scratch_shapes=[pltpu.SemaphoreType.DMA]),
        out_shape=jax.ShapeDtypeStruct(xs_init.shape, F32),
        input_output_aliases={2: 0},
        compiler_params=_cparams(("arbitrary",)),
        name="moe_dispatch",
    )(pos, hs, xs_init)


def _experts_kernel(te_ref, nv_ref, dest_ref, xs_ref, wg_ref, wu_ref, wd_ref, yi_ref, yt_ref,
                    wgb, wub, wdb, yblk, sem):
    del yi_ref
    j = pl.program_id(0)
    nv = nv_ref[0]
    tm = xs_ref.shape[0] // SLAB

    def wait_rows():
        pltpu.make_async_copy(yblk, yt_ref.at[pl.ds(0, tm * SLAB), :], sem).wait()

    @pl.when(j < nv)
    def _():
        prev = te_ref[jnp.maximum(j - 1, 0)]

        @pl.when((j == 0) | (prev != te_ref[j]))
        def _():
            wgb[...] = wg_ref[0].astype(BF16)
            wub[...] = wu_ref[0].astype(BF16)
            wdb[...] = wd_ref[0].astype(BF16)

        x = jnp.concatenate([xs_ref[pl.ds(s, tm, stride=SLAB), :] for s in range(SLAB)],
                            axis=1).astype(BF16)
        hg = jnp.dot(x, wgb[...], preferred_element_type=F32)
        hu = jnp.dot(x, wub[...], preferred_element_type=F32)
        y = _dot(_silu(hg) * hu, wdb[...])

        @pl.when(j > 0)
        def _():
            wait_rows()

        for s in range(SLAB):
            yblk[pl.ds(s, tm, stride=SLAB), :] = y[:, s * HEAD_DIM:(s + 1) * HEAD_DIM]

        def issue(r, carry):
            d = dest_ref[j * tm + r]
            pltpu.make_async_copy(yblk.at[_slab_rows(r), :], yt_ref.at[_slab_rows(d), :], sem).start()
            return carry

        lax.fori_loop(0, tm, issue, 0, unroll=8)

        @pl.when(j == nv - 1)
        def _():
            wait_rows()


def grouped_experts(tile_expert, n_valid, dest, xs, w_gate, w_up, w_down, yt_init, tm=MOE_TM):
    ne, d, dff = w_gate.shape
    n_tiles = tile_expert.shape[0]

    def row_tile(j, te, nv, dest):
        return (jnp.minimum(j, nv[0] - 1), 0)

    def expert(j, te, nv, dest):
        return (te[jnp.minimum(j, nv[0] - 1)], 0, 0)

    return pl.pallas_call(
        _experts_kernel,
        grid_spec=pltpu.PrefetchScalarGridSpec(
            num_scalar_prefetch=3,
            grid=(n_tiles,),
            in_specs=[pl.BlockSpec((tm * SLAB, HEAD_DIM), row_tile),
                      pl.BlockSpec((1, d, dff), expert),
                      pl.BlockSpec((1, d, dff), expert),
                      pl.BlockSpec((1, dff, d), expert),
                      pl.BlockSpec(memory_space=pl.ANY)],
            out_specs=pl.BlockSpec(memory_space=pl.ANY),
            scratch_shapes=[pltpu.VMEM((d, dff), BF16), pltpu.VMEM((d, dff), BF16),
                            pltpu.VMEM((dff, d), BF16), pltpu.VMEM((tm * SLAB, HEAD_DIM), F32),
                            pltpu.SemaphoreType.DMA]),
        out_shape=jax.ShapeDtypeStruct(yt_init.shape, F32),
        input_output_aliases={7: 0},
        compiler_params=_cparams(("arbitrary",)),
        name="moe_experts",
    )(tile_expert, n_valid, dest, xs, w_gate, w_up, w_down, yt_init)


def _combine_kernel(x_ref, y1_ref, y2_ref, w_ref, gate_ref, o_ref):
    tm = x_ref.shape[1]
    w = w_ref[...]
    w1 = w[:, 0:1]
    w2 = w[:, 1:2]
    parts = [w1 * y1_ref[pl.ds(s, tm, stride=SLAB), :] + w2 * y2_ref[pl.ds(s, tm, stride=SLAB), :]
             for s in range(SLAB)]
    o_ref[0] = x_ref[0] + gate_ref[0] * jnp.concatenate(parts, axis=1)


def combine(x, yt, w_cols, gate, tm=256):
    b, s, d = x.shape
    nt = s // tm
    t = b * s
    return pl.pallas_call(
        _combine_kernel,
        grid=(b, nt),
        in_specs=[pl.BlockSpec((1, tm, d), lambda bi, i: (bi, i, 0)),
                  pl.BlockSpec((tm * SLAB, HEAD_DIM), lambda bi, i: (bi * nt + i, 0)),
                  pl.BlockSpec((tm * SLAB, HEAD_DIM), lambda bi, i: (t // tm + bi * nt + i, 0)),
                  pl.BlockSpec((tm, META_ROWS), lambda bi, i: (bi * nt + i, 0)),
                  pl.BlockSpec((1, 1, d), lambda bi, i: (bi, 0, 0))],
        out_specs=pl.BlockSpec((1, tm, d), lambda bi, i: (bi, i, 0)),
        out_shape=jax.ShapeDtypeStruct((b, s, d), F32),
        compiler_params=_cparams(("arbitrary", "arbitrary")),
        name="moe_combine",
    )(x, yt, yt, w_cols, gate)


def _final_norm_kernel(x_ref, nw_ref, o_ref):
    x = x_ref[0]
    ms = jnp.mean(x * x, axis=-1, keepdims=True)
    o_ref[0] = x * lax.rsqrt(ms + EPS) * nw_ref[...]


def final_norm(x, norm_w, tm=1024):
    b, s, d = x.shape
    return pl.pallas_call(
        _final_norm_kernel,
        grid=(b, s // tm),
        in_specs=[pl.BlockSpec((1, tm, d), lambda bi, i: (bi, i, 0)),
                  pl.BlockSpec((1, d), lambda bi, i: (0, 0))],
        out_specs=pl.BlockSpec((1, tm, d), lambda bi, i: (bi, i, 0)),
        out_shape=jax.ShapeDtypeStruct((b, s, d), F32),
        compiler_params=_cparams(("arbitrary", "arbitrary")),
        name="final_norm",
    )(x, norm_w)


def _split_w_in(w_in):
    gw = GROUP_WIDTH
    a_end = 3 * gw
    b_end = a_end + 2 * gw
    c_main_end = b_end + 4 * gw
    c_end = c_main_end + 2 * N_HEADS
    d_main_end = c_end + 4 * gw
    main = jnp.concatenate([w_in[:, :c_main_end], w_in[:, c_end:d_main_end]], axis=1)
    small = jnp.concatenate([w_in[:, c_main_end:c_end], w_in[:, d_main_end:]], axis=1)
    small = jnp.pad(small, ((0, 0), (0, GATE_LANES - small.shape[1])))
    return main.astype(BF16), small.astype(BF16)


def _lanes(vec, offset):
    return jnp.zeros((1, GATE_LANES), F32).at[0, offset:offset + N_HEADS].set(vec)


def _rope_tables(s):
    half = HEAD_DIM // 2
    inv_freq = ROPE_THETA ** (-jnp.arange(half, dtype=F32) / half)
    ang = jnp.arange(s, dtype=F32)[:, None] * inv_freq[None, :]
    cos, sin = jnp.cos(ang), jnp.sin(ang)
    return jnp.concatenate([cos, cos], axis=1), jnp.concatenate([-sin, sin], axis=1)


def hybrid_mixer_layer(x, sc1, sh1, g1, norm_w, w_in, w_out, gn_a, conv_b_w, conv_b_b, w_rg, b_rg,
                       w_ig, b_ig, lru_lambda, gn_b, conv_c_w, gdn_a_log, gdn_dt_bias, gdn_norm,
                       mlstm_i_bias, mlstm_f_bias, gn_d, rope):
    w_main, w_small = _split_w_in(w_in)
    proj, gates = in_projection(x, sc1, sh1, norm_w.reshape(1, -1), w_main, w_small)
    row = lambda v: v.reshape(1, -1)
    y_a = dilated_attention_mixer(proj, rope[0], rope[1], row(gn_a))
    y_b = rg_lru_mixer(proj, conv_b_w, row(conv_b_b), w_rg.astype(BF16), row(b_rg),
                       w_ig.astype(BF16), row(b_ig), row(lru_lambda), row(gn_b))
    y_c = gated_delta_mixer(proj, gates, conv_c_w, _lanes(gdn_a_log, N_HEADS),
                            _lanes(gdn_dt_bias, N_HEADS), row(jnp.tile(gdn_norm, N_HEADS)))
    y_d = mlstm_mixer(proj, gates, _lanes(mlstm_i_bias, 2 * N_HEADS),
                      _lanes(mlstm_f_bias, 3 * N_HEADS), row(gn_d))
    return out_projection(x, (y_a, y_b, y_c, y_d), w_out.astype(BF16), g1)


def moe_layer(x, sc2, sh2, g2, norm_w, w_router, router_bias, w_gate, w_up, w_down, xs_buf, yt_buf):
    rb = jnp.broadcast_to(router_bias.reshape(-1, 1), (N_EXPERTS, GATE_LANES))
    hs, meta, wts, counts = router(x, sc2, sh2, norm_w.reshape(1, -1), w_router.T, rb)
    pos, tile_expert, n_valid, dest = routing_tables(meta, counts)
    xs = dispatch(pos, hs, xs_buf)
    yt = grouped_experts(tile_expert, n_valid, dest, xs, w_gate, w_up, w_down, yt_buf)
    return combine(x, yt, wts.T, g2), xs, yt


def moe_buffers(n_tokens):
    rows = 2 * n_tokens + N_EXPERTS * MOE_TM
    return jnp.zeros((rows * SLAB, HEAD_DIM), F32), jnp.zeros((rows * SLAB, HEAD_DIM), F32)


def kernel(x, c, norm_mix, norm_ffn, norm_final, w_ada, b_ada, w_in, w_out, gn_a, conv_b_w, conv_b_b, w_rg, b_rg, w_ig, b_ig, lru_lambda, gn_b, conv_c_w, gdn_a_log, gdn_dt_bias, gdn_norm, mlstm_i_bias, mlstm_f_bias, gn_d, w_router, router_bias, moe_w_gate, moe_w_up, moe_w_down):
    depth = w_ada.shape[0]
    b, s, d = x.shape
    mod = ada_modulation(c, w_ada, b_ada)
    rope = _rope_tables(s)
    xs_buf, yt_buf = moe_buffers(b * s)
    for l in range(depth):
        sh1, sc1, g1, sh2, sc2, g2 = [mod[l, :, None, i * d:(i + 1) * d] for i in range(6)]
        x = hybrid_mixer_layer(x, sc1, sh1, g1, norm_mix[l], w_in[l], w_out[l], gn_a[l],
                               conv_b_w[l], conv_b_b[l], w_rg[l], b_rg[l], w_ig[l], b_ig[l],
                               lru_lambda[l], gn_b[l], conv_c_w[l], gdn_a_log[l], gdn_dt_bias[l],
                               gdn_norm[l], mlstm_i_bias[l], mlstm_f_bias[l], gn_d[l], rope)
        x, xs_buf, yt_buf = moe_layer(x, sc2, sh2, g2, norm_ffn[l], w_router, router_bias,
                                      moe_w_gate[l], moe_w_up[l], moe_w_down[l], xs_buf, yt_buf)
    return final_norm(x, norm_final.reshape(1, -1))
```

```python
import functools
import math

import jax
import jax.numpy as jnp
from jax import lax
from jax.experimental import pallas as pl
from jax.experimental.pallas import tpu as pltpu

F32 = jnp.float32
BF16 = jnp.bfloat16
HIGHEST = lax.Precision.HIGHEST

HEAD_DIM = 128
N_HEADS = 4
GROUP_WIDTH = N_HEADS * HEAD_DIM
DILATED_PAIRS = ((128, 1), (512, 4), (2048, 16))
ATTN_BLOCK = 128
ROPE_THETA = 10000.0
CONV_WIDTH = 4
LRU_C = 8.0
CHUNK = 64
N_EXPERTS = 16
EXPERTS_PER_GROUP = 4
EPS = 1e-6
M_INIT = -1e30
NEG = -1e30
SUBLANES = 8
VMEM_LIMIT = 56 * 1024 * 1024

COL_AQ, COL_AK, COL_AV = 0, 4, 8
COL_BX, COL_BG = 3, 4
COL_CQ, COL_CK, COL_CV, COL_CZ = 5, 6, 7, 8
COL_DQ, COL_DK, COL_DV, COL_DO = 9, 10, 11, 12
MAIN_COLS = 13 * GROUP_WIDTH
GATE_LANES = 128


def _cparams(sem):
    return pltpu.CompilerParams(dimension_semantics=sem, vmem_limit_bytes=VMEM_LIMIT)


def _dot(a, b):
    return jnp.dot(a.astype(BF16), b.astype(BF16), preferred_element_type=F32)


def _dot_nt(a, b):
    return lax.dot_general(a.astype(BF16), b.astype(BF16), (((1,), (1,)), ((), ())),
                           preferred_element_type=F32)


def _dot_tn(a, b):
    return lax.dot_general(a.astype(BF16), b.astype(BF16), (((0,), (0,)), ((), ())),
                           preferred_element_type=F32)


def _dot_exact(a, b):
    return jnp.dot(a, b, precision=HIGHEST, preferred_element_type=F32)


def _row_form(col_b):
    c = col_b.shape[0]
    lane = lax.broadcasted_iota(jnp.int32, col_b.shape, 1)
    picked = jnp.where(lane == 0, col_b, 0.0)
    return lax.dot_general(jnp.ones((c, col_b.shape[1]), F32), picked, (((1,), (1,)), ((), ())),
                           precision=HIGHEST, preferred_element_type=F32)


def _softplus(z):
    return jnp.maximum(z, 0.0) + jnp.log1p(jnp.exp(-jnp.abs(z)))


def _sigmoid(z):
    return 1.0 / (1.0 + jnp.exp(-z))


def _silu(z):
    return z * _sigmoid(z)


def _head_rms(x, gain):
    outs = []
    for g in range(x.shape[1] // HEAD_DIM):
        xg = x[:, g * HEAD_DIM:(g + 1) * HEAD_DIM]
        ms = jnp.mean(xg * xg, axis=-1, keepdims=True)
        outs.append(xg * lax.rsqrt(ms + EPS))
    y = outs[0] if len(outs) == 1 else jnp.concatenate(outs, axis=1)
    return y * gain


def _tril(c, strict=False):
    row = lax.broadcasted_iota(jnp.int32, (c, c), 0)
    col = lax.broadcasted_iota(jnp.int32, (c, c), 1)
    return (col < row) if strict else (col <= row)


def _ada_kernel(c_ref, w_ref, b_ref, o_ref):
    c = c_ref[...]
    o_ref[0] = _dot(_silu(c), w_ref[0]) + b_ref[0]


def ada_modulation(c, w_ada, b_ada):
    depth, d, n = w_ada.shape
    b = c.shape[0]
    rows = -(-b // SUBLANES) * SUBLANES
    c_pad = jnp.zeros((rows, d), F32).at[:b].set(c)
    tn = 1024
    out = pl.pallas_call(
        _ada_kernel,
        grid=(depth, n // tn),
        in_specs=[pl.BlockSpec((rows, d), lambda l, j: (0, 0)),
                  pl.BlockSpec((1, d, tn), lambda l, j: (l, 0, j)),
                  pl.BlockSpec((1, 1, tn), lambda l, j: (l, 0, j))],
        out_specs=pl.BlockSpec((1, rows, tn), lambda l, j: (l, 0, j)),
        out_shape=jax.ShapeDtypeStruct((depth, rows, n), F32),
        compiler_params=_cparams(("arbitrary", "arbitrary")),
        name="ada_modulation",
    )(c_pad, w_ada, b_ada.reshape(depth, 1, n))
    return out[:, :b]


def _modulated_norm(x, nw, sc, sh):
    ms = jnp.mean(x * x, axis=-1, keepdims=True)
    return (x * lax.rsqrt(ms + EPS) * nw) * (1.0 + sc) + sh


def _in_proj_kernel(x_ref, sc_ref, sh_ref, nw_ref, w_ref, wg_ref, proj_ref, gates_ref, h_scr):
    @pl.when(pl.program_id(2) == 0)
    def _():
        h = _modulated_norm(x_ref[0], nw_ref[...], sc_ref[0], sh_ref[0]).astype(BF16)
        h_scr[...] = h
        gates_ref[0] = jnp.dot(h, wg_ref[...], preferred_element_type=F32)

    proj_ref[0] = jnp.dot(h_scr[...], w_ref[...], preferred_element_type=F32).astype(BF16)


def in_projection(x, scale, shift, norm_w, w_main, w_gate, tm=512, tn=1664):
    b, s, d = x.shape
    n = w_main.shape[1]
    return pl.pallas_call(
        _in_proj_kernel,
        grid=(b, s // tm, n // tn),
        in_specs=[pl.BlockSpec((1, tm, d), lambda bi, i, j: (bi, i, 0)),
                  pl.BlockSpec((1, 1, d), lambda bi, i, j: (bi, 0, 0)),
                  pl.BlockSpec((1, 1, d), lambda bi, i, j: (bi, 0, 0)),
                  pl.BlockSpec((1, d), lambda bi, i, j: (0, 0)),
                  pl.BlockSpec((d, tn), lambda bi, i, j: (0, j)),
                  pl.BlockSpec((d, GATE_LANES), lambda bi, i, j: (0, 0))],
        out_specs=[pl.BlockSpec((1, tm, tn), lambda bi, i, j: (bi, i, j)),
                   pl.BlockSpec((1, tm, GATE_LANES), lambda bi, i, j: (bi, i, 0))],
        out_shape=[jax.ShapeDtypeStruct((b, s, n), BF16),
                   jax.ShapeDtypeStruct((b, s, GATE_LANES), F32)],
        scratch_shapes=[pltpu.VMEM((tm, d), BF16)],
        compiler_params=_cparams(("arbitrary", "arbitrary", "arbitrary")),
        name="in_projection",
    )(x, scale, shift, norm_w, w_main, w_gate)


def _out_proj_kernel(x_ref, ya_ref, yb_ref, yc_ref, yd_ref, w_ref, g_ref, o_ref):
    acc = None
    for k, y_ref in enumerate((ya_ref, yb_ref, yc_ref, yd_ref)):
        part = jnp.dot(y_ref[0], w_ref[k * GROUP_WIDTH:(k + 1) * GROUP_WIDTH, :],
                       preferred_element_type=F32)
        acc = part if acc is None else acc + part
    o_ref[0] = x_ref[0] + g_ref[0] * acc


def out_projection(x, ys, w_out, gate, tm=512):
    b, s, d = x.shape
    yspec = pl.BlockSpec((1, tm, GROUP_WIDTH), lambda bi, i: (bi, i, 0))
    return pl.pallas_call(
        _out_proj_kernel,
        grid=(b, s // tm),
        in_specs=[pl.BlockSpec((1, tm, d), lambda bi, i: (bi, i, 0)),
                  yspec, yspec, yspec, yspec,
                  pl.BlockSpec(w_out.shape, lambda bi, i: (0, 0)),
                  pl.BlockSpec((1, 1, d), lambda bi, i: (bi, 0, 0))],
        out_specs=pl.BlockSpec((1, tm, d), lambda bi, i: (bi, i, 0)),
        out_shape=jax.ShapeDtypeStruct((b, s, d), F32),
        compiler_params=_cparams(("arbitrary", "arbitrary")),
        name="out_projection",
    )(x, *ys, w_out, gate)


def _attn_kernel(q_ref, k_ref, v_ref, cos_ref, sin_ref, gn_ref, o_ref, qs, ks, vs, acc, ms, ls):
    s = q_ref.shape[1]
    blk = ATTN_BLOCK
    cos = cos_ref[...]
    sin = sin_ref[...]

    def rot(t):
        return t * cos + pltpu.roll(t, HEAD_DIM // 2, 1) * sin

    qs[...] = rot(q_ref[0].astype(F32)) * (1.0 / math.sqrt(HEAD_DIM))
    ks[...] = rot(k_ref[0].astype(F32))
    vs[...] = v_ref[0].astype(F32)

    row = lax.broadcasted_iota(jnp.int32, (blk, 2 * blk), 0)
    col = lax.broadcasted_iota(jnp.int32, (blk, 2 * blk), 1)
    own_ok = (col >= blk) & (col - blk <= row)

    for branch, (window, dil) in enumerate(DILATED_PAIRS):
        nb = s // (blk * dil)

        def rows(start, dil=dil):
            if dil == 1:
                return pl.ds(pl.multiple_of(start, blk), blk)
            return pl.ds(start, blk, stride=dil)

        def body(idx, carry, dil=dil, nb=nb, branch=branch, rows=rows):
            r = idx // nb
            n = idx % nb
            start = r + n * (blk * dil)
            pstart = jnp.where(n > 0, start - blk * dil, start)
            qb = qs[rows(start), :]
            kc = jnp.concatenate([ks[rows(pstart), :], ks[rows(start), :]], axis=0)
            vc = jnp.concatenate([vs[rows(pstart), :], vs[rows(start), :]], axis=0)
            sc = _dot_nt(qb, kc)
            first = jnp.where(n > 0, 0, 4 * blk)
            valid = own_ok | ((col < blk) & (col >= row + first))
            sc = jnp.where(valid, sc, NEG)
            mb = jnp.max(sc, axis=-1, keepdims=True)
            p = jnp.exp(sc - mb)
            den = jnp.sum(p, axis=-1, keepdims=True)
            num = _dot(p, vc)
            mb = jnp.broadcast_to(mb, (blk, HEAD_DIM))
            den = jnp.broadcast_to(den, (blk, HEAD_DIM))
            if branch == 0:
                acc[rows(start), :] = num
                ms[rows(start), :] = mb
                ls[rows(start), :] = den
            else:
                m_old = ms[rows(start), :]
                m_new = jnp.maximum(m_old, mb)
                a_old = jnp.exp(m_old - m_new)
                a_blk = jnp.exp(mb - m_new)
                acc[rows(start), :] = acc[rows(start), :] * a_old + num * a_blk
                ls[rows(start), :] = ls[rows(start), :] * a_old + den * a_blk
                ms[rows(start), :] = m_new
            return carry

        lax.fori_loop(0, s // blk, body, 0, unroll=8)

    out = acc[...] / ls[...]
    o_ref[0] = _head_rms(out, gn_ref[...]).astype(o_ref.dtype)


def dilated_attention_mixer(proj, cos2, sin2, gn_a):
    b, s, _ = proj.shape
    hd = HEAD_DIM

    def col(off):
        return pl.BlockSpec((1, s, hd), lambda bi, h, off=off: (bi, 0, off + h))

    return pl.pallas_call(
        _attn_kernel,
        grid=(b, N_HEADS),
        in_specs=[col(COL_AQ), col(COL_AK), col(COL_AV),
                  pl.BlockSpec((s, hd), lambda bi, h: (0, 0)),
                  pl.BlockSpec((s, hd), lambda bi, h: (0, 0)),
                  pl.BlockSpec((1, hd), lambda bi, h: (0, h))],
        out_specs=pl.BlockSpec((1, s, hd), lambda bi, h: (bi, 0, h)),
        out_shape=jax.ShapeDtypeStruct((b, s, GROUP_WIDTH), BF16),
        scratch_shapes=[pltpu.VMEM((s, hd), F32) for _ in range(6)],
        compiler_params=_cparams(("arbitrary", "arbitrary")),
        name="dilated_attention",
    )(proj, proj, proj, cos2, sin2, gn_a)


def _load_conv_tile(x_ref, xpad, ts):
    @pl.when(pl.program_id(1) == 0)
    def _():
        xpad[0:SUBLANES, :] = jnp.zeros((SUBLANES, xpad.shape[1]), F32)

    @pl.when(pl.program_id(1) > 0)
    def _():
        xpad[0:SUBLANES, :] = xpad[ts:ts + SUBLANES, :]

    xpad[SUBLANES:SUBLANES + ts, :] = x_ref[0].astype(F32)


def _conv_from_pad(xpad, w, ts):
    y = None
    for j in range(CONV_WIDTH):
        off = SUBLANES - (CONV_WIDTH - 1) + j
        term = xpad[off:off + ts, :] * w[j:j + 1, :]
        y = term if y is None else y + term
    return y


def _lru_kernel(x_ref, g_ref, cw_ref, cb_ref, wr_ref, br_ref, wi_ref, bi_ref, lam_ref, gn_ref,
                o_ref, xpad, a_scr, u_scr, carry):
    ts = x_ref.shape[1]
    w = x_ref.shape[2]
    _load_conv_tile(x_ref, xpad, ts)

    @pl.when(pl.program_id(1) == 0)
    def _():
        carry[...] = jnp.zeros(carry.shape, F32)

    xc = _conv_from_pad(xpad, cw_ref[...], ts) + cb_ref[...]
    r_parts, i_parts = [], []
    for g in range(N_HEADS):
        xg = xc[:, g * HEAD_DIM:(g + 1) * HEAD_DIM]
        r_parts.append(_dot(xg, wr_ref[g]))
        i_parts.append(_dot(xg, wi_ref[g]))
    r = _sigmoid(jnp.concatenate(r_parts, axis=1) + br_ref[...])
    ig = _sigmoid(jnp.concatenate(i_parts, axis=1) + bi_ref[...])
    log_a = (-LRU_C) * r * _softplus(-lam_ref[...])
    a_scr[...] = jnp.exp(log_a)
    u_scr[...] = jnp.sqrt(1.0 - jnp.exp(2.0 * log_a)) * (ig * xc)

    sub = lax.broadcasted_iota(jnp.int32, (SUBLANES, w), 0)

    def body(j, h_prev):
        rows = pl.ds(pl.multiple_of(j * SUBLANES, SUBLANES), SUBLANES)
        a = a_scr[rows, :]
        u = u_scr[rows, :]
        for sh in (1, 2, 4):
            keep = sub >= sh
            a_sh = jnp.where(keep, pltpu.roll(a, sh, 0), 1.0)
            u_sh = jnp.where(keep, pltpu.roll(u, sh, 0), 0.0)
            u = a * u_sh + u
            a = a * a_sh
        h = u + a * h_prev
        u_scr[rows, :] = h
        return jnp.broadcast_to(h[SUBLANES - 1:SUBLANES, :], (SUBLANES, w))

    carry[...] = lax.fori_loop(0, ts // SUBLANES, body, carry[...])
    y = _head_rms(u_scr[...], gn_ref[...]) * jax.nn.gelu(g_ref[0].astype(F32))
    o_ref[0] = y.astype(o_ref.dtype)


def rg_lru_mixer(proj, conv_w, conv_b, w_rg, b_rg, w_ig, b_ig, lam, gn_b, ts=1024):
    b, s, _ = proj.shape
    w = GROUP_WIDTH
    vec = pl.BlockSpec((1, w), lambda bi, i: (0, 0))
    blockdiag = pl.BlockSpec((N_HEADS, HEAD_DIM, HEAD_DIM), lambda bi, i: (0, 0, 0))
    return pl.pallas_call(
        _lru_kernel,
        grid=(b, s // ts),
        in_specs=[pl.BlockSpec((1, ts, w), lambda bi, i: (bi, i, COL_BX)),
                  pl.BlockSpec((1, ts, w), lambda bi, i: (bi, i, COL_BG)),
                  pl.BlockSpec((CONV_WIDTH, w), lambda bi, i: (0, 0)),
                  vec, blockdiag, vec, blockdiag, vec, vec, vec],
        out_specs=pl.BlockSpec((1, ts, w), lambda bi, i: (bi, i, 0)),
        out_shape=jax.ShapeDtypeStruct((b, s, w), BF16),
        scratch_shapes=[pltpu.VMEM((ts + SUBLANES, w), F32), pltpu.VMEM((ts, w), F32),
                        pltpu.VMEM((ts, w), F32), pltpu.VMEM((SUBLANES, w), F32)],
        compiler_params=_cparams(("arbitrary", "arbitrary")),
        name="rg_lru",
    )(proj, proj, conv_w, conv_b, w_rg, b_rg, w_ig, b_ig, lam, gn_b)


def _lane_pick(x, lane_idx):
    lane = lax.broadcasted_iota(jnp.int32, x.shape, 1)
    col = jnp.sum(jnp.where(lane == lane_idx, x, 0.0), axis=-1, keepdims=True)
    return jnp.broadcast_to(col, x.shape)


def _bdot(a, b):
    return jnp.einsum("nij,njk->nik", a.astype(BF16), b.astype(BF16), preferred_element_type=F32)


def _bdot_nt(a, b):
    return jnp.einsum("nid,njd->nij", a.astype(BF16), b.astype(BF16), preferred_element_type=F32)


def _bdot_tn(a, b):
    return jnp.einsum("ncd,nce->nde", a.astype(BF16), b.astype(BF16), preferred_element_type=F32)


def _chunk_cumsum(x, c):
    nc = x.shape[0] // c
    tri = jnp.broadcast_to(_tril(c).astype(F32), (nc, c, c))
    return jnp.einsum("nij,njk->nik", tri, x.reshape(nc, c, x.shape[1]), precision=HIGHEST,
                      preferred_element_type=F32)


def _row_forms(x3, specs):
    nc, c, w = x3.shape
    rows = len(specs) * c
    lane = lax.broadcasted_iota(jnp.int32, (rows, w), 1)
    row = lax.broadcasted_iota(jnp.int32, (rows, w), 0)
    sel = jnp.zeros((rows, w), F32)
    for i, spec in enumerate(specs):
        in_rows = (row >= i * c) & (row < (i + 1) * c)
        for ln, coeff in spec:
            sel = jnp.where(in_rows & (lane == ln), coeff, sel)
    out = jnp.einsum("nil,njl->nij", jnp.broadcast_to(sel, (nc, rows, w)), x3, precision=HIGHEST,
                     preferred_element_type=F32)
    return [out[:, i * c:(i + 1) * c, :] for i in range(len(specs))]


def _lane_pick3(x3, lane_idx):
    lane = lax.broadcasted_iota(jnp.int32, x3.shape, 2)
    col = jnp.sum(jnp.where(lane == lane_idx, x3, 0.0), axis=-1, keepdims=True)
    return jnp.broadcast_to(col, x3.shape[:2] + (HEAD_DIM,))


def _gdn_kernel(q_ref, k_ref, v_ref, z_ref, gt_ref, cwq_ref, cwk_ref, cwv_ref, alog_ref, dtb_ref,
                gn_ref, o_ref, qpad, kpad, vpad, p_scr, n_scr, qp_scr, op_scr, gl_scr, o_scr, state):
    ts = q_ref.shape[1]
    c = CHUNK
    hd = HEAD_DIM
    nc = ts // c
    _load_conv_tile(q_ref, qpad, ts)
    _load_conv_tile(k_ref, kpad, ts)
    _load_conv_tile(v_ref, vpad, ts)

    @pl.when(pl.program_id(1) == 0)
    def _():
        state[...] = jnp.zeros(state.shape, F32)

    qc = _silu(_conv_from_pad(qpad, cwq_ref[...], ts))
    kc = _silu(_conv_from_pad(kpad, cwk_ref[...], ts))
    vc = _silu(_conv_from_pad(vpad, cwv_ref[...], ts))
    gates = gt_ref[0]
    neg_a = -jnp.exp(alog_ref[...])
    g_all = neg_a * _softplus(gates + dtb_ref[...])
    beta3_all = _sigmoid(gates).reshape(nc, c, GATE_LANES)
    gc3_all = _chunk_cumsum(g_all, c)
    gc_rows = _row_forms(gc3_all, [[(N_HEADS + h, 1.0)] for h in range(N_HEADS)])
    incl = _tril(c)
    strict = _tril(c, strict=True)
    eye = (lax.broadcasted_iota(jnp.int32, (c, c), 0)
           == lax.broadcasted_iota(jnp.int32, (c, c), 1)).astype(F32)

    q_u, k_u, kb_u, vb_u, kbe_u, qd_u, kd_u, dec_u, gl_u = ([] for _ in range(9))
    for h in range(N_HEADS):
        hs = slice(h * hd, (h + 1) * hd)
        qh = qc[:, hs]
        kh = kc[:, hs]
        qn = (qh * lax.rsqrt(jnp.sum(qh * qh, axis=-1, keepdims=True) + EPS)
              * (hd ** -0.5)).reshape(nc, c, hd)
        kn = (kh * lax.rsqrt(jnp.sum(kh * kh, axis=-1, keepdims=True) + EPS)).reshape(nc, c, hd)
        beta = _lane_pick3(beta3_all, h)
        gc = _lane_pick3(gc3_all, N_HEADS + h)
        e_gc = jnp.exp(gc)
        gc_last = gc[:, c - 1:c, :]
        kb = kn * beta
        q_u.append(qn.astype(BF16))
        k_u.append(kn.astype(BF16))
        kb_u.append(kb.astype(BF16))
        vb_u.append((vc[:, hs].reshape(nc, c, hd) * beta).astype(BF16))
        kbe_u.append((kb * e_gc).astype(BF16))
        qd_u.append(qn * e_gc)
        kd_u.append((kn * jnp.exp(gc_last - gc)).astype(BF16))
        dec_u.append(jnp.exp(jnp.where(incl, gc[:, :, :c] - gc_rows[h], NEG)))
        gl_u.append(jnp.exp(gc_last))
    cat = lambda parts: jnp.concatenate(parts, axis=0)
    q_u, k_u, kb_u, vb_u, kbe_u, qd_u, kd_u, dec_u = (
        cat(t) for t in (q_u, k_u, kb_u, vb_u, kbe_u, qd_u, kd_u, dec_u))
    gl_scr[...] = cat(gl_u)

    a_mat = jnp.where(strict, _bdot_nt(kb_u, k_u) * dec_u, 0.0)
    t_inv = eye - a_mat
    a_pow = a_mat
    for _ in range(int(math.log2(c)) - 1):
        a_pow = _bdot(a_pow, a_pow)
        t_inv = t_inv + _bdot(t_inv, a_pow)
    u = _bdot(t_inv, vb_u)
    w = _bdot(t_inv, kbe_u)
    qk = jnp.where(incl, _bdot_nt(q_u, k_u) * dec_u, 0.0)
    p_scr[...] = _bdot_tn(kd_u, w).astype(BF16)
    n_scr[...] = _bdot_tn(kd_u, u)
    qp_scr[...] = (qd_u - _bdot(qk, w)).astype(BF16)
    op_scr[...] = _bdot(qk, u)

    def step(ci, carry):
        rows = pl.ds(pl.multiple_of(ci * c, c), c)
        for h in range(N_HEADS):
            hs = slice(h * hd, (h + 1) * hd)
            idx = h * nc + ci
            st = state[h]
            stb = st.astype(BF16)
            o_scr[rows, hs] = jnp.dot(qp_scr[idx], stb, preferred_element_type=F32) + op_scr[idx]
            state[h] = (gl_scr[idx] * st - jnp.dot(p_scr[idx], stb, preferred_element_type=F32)
                        + n_scr[idx])
        return carry

    lax.fori_loop(0, nc, step, 0)
    y = _head_rms(o_scr[...], gn_ref[...]) * _silu(z_ref[0].astype(F32))
    o_ref[0] = y.astype(o_ref.dtype)


def gated_delta_mixer(proj, gates, conv_w, a_log_v, dt_bias_v, gn_c, ts=512):
    b, s, _ = proj.shape
    w = GROUP_WIDTH
    hd = HEAD_DIM
    units = N_HEADS * (ts // CHUNK)

    def col(off):
        return pl.BlockSpec((1, ts, w), lambda bi, i, off=off: (bi, i, off))

    def cw(off):
        return pl.BlockSpec((CONV_WIDTH, w), lambda bi, i, off=off: (0, off))

    vec = pl.BlockSpec((1, GATE_LANES), lambda bi, i: (0, 0))
    big = lambda: pltpu.VMEM((ts, w), F32)
    return pl.pallas_call(
        _gdn_kernel,
        grid=(b, s // ts),
        in_specs=[col(COL_CQ), col(COL_CK), col(COL_CV), col(COL_CZ),
                  pl.BlockSpec((1, ts, GATE_LANES), lambda bi, i: (bi, i, 0)),
                  cw(0), cw(1), cw(2), vec, vec,
                  pl.BlockSpec((1, w), lambda bi, i: (0, 0))],
        out_specs=pl.BlockSpec((1, ts, w), lambda bi, i: (bi, i, 0)),
        out_shape=jax.ShapeDtypeStruct((b, s, w), BF16),
        scratch_shapes=[pltpu.VMEM((ts + SUBLANES, w), F32) for _ in range(3)]
        + [pltpu.VMEM((units, hd, hd), BF16), pltpu.VMEM((units, hd, hd), F32),
           pltpu.VMEM((units, CHUNK, hd), BF16), pltpu.VMEM((units, CHUNK, hd), F32),
           pltpu.VMEM((units, 1, hd), F32), big(),
           pltpu.VMEM((N_HEADS, hd, hd), F32)],
        compiler_params=_cparams(("arbitrary", "arbitrary")),
        name="gated_delta_net",
    )(proj, proj, proj, proj, gates, conv_w, conv_w, conv_w, a_log_v, dt_bias_v, gn_c)


def _mlstm_kernel(q_ref, k_ref, v_ref, og_ref, gt_ref, ib_ref, fb_ref, gn_ref, o_ref,
                  kvn_scr, keep_scr, qcn_scr, h_scr, cn_state, m_state):
    ts = q_ref.shape[1]
    c = CHUNK
    hd = HEAD_DIM
    nc = ts // c

    @pl.when(pl.program_id(1) == 0)
    def _():
        cn_state[...] = jnp.zeros(cn_state.shape, F32)
        m_state[...] = jnp.full(m_state.shape, M_INIT, F32)

    gates = gt_ref[0]
    i_all = gates + ib_ref[...]
    f_pre = gates + fb_ref[...]
    lf_all = jnp.minimum(f_pre, 0.0) - jnp.log1p(jnp.exp(-jnp.abs(f_pre)))
    b3_all = _chunk_cumsum(lf_all, c)
    lane = lax.broadcasted_iota(jnp.int32, (nc, c, GATE_LANES), 2)
    ib3_all = jnp.where(lane < 3 * N_HEADS, i_all.reshape(nc, c, GATE_LANES), b3_all)
    ib_rows = _row_forms(ib3_all, [[(2 * N_HEADS + h, 1.0), (3 * N_HEADS + h, -1.0)]
                                   for h in range(N_HEADS)])
    incl = _tril(c)
    k_scale = 1.0 / math.sqrt(hd)
    ones_v = jnp.ones((nc, c, hd), BF16)

    per_head = []
    for h in range(N_HEADS):
        hs = slice(h * hd, (h + 1) * hd)
        q3 = q_ref[0, :, hs].reshape(nc, c, hd)
        k3 = k_ref[0, :, hs].astype(F32).reshape(nc, c, hd) * k_scale
        v3 = v_ref[0, :, hs].reshape(nc, c, hd)
        bb = _lane_pick3(b3_all, 3 * N_HEADS + h)
        ig = _lane_pick3(ib3_all, 2 * N_HEADS + h)
        log_d = jnp.where(incl, bb[:, :, :c] + ib_rows[h], NEG)
        max_ld = jnp.max(log_d, axis=-1, keepdims=True)
        b_last = bb[:, c - 1:c, :]
        log_w = b_last - bb + ig
        max_lw = jnp.max(log_w, axis=1, keepdims=True)
        m = m_state[h]
        m_prev, m_next = [], []
        for ci in range(nc):
            m_prev.append(m)
            m = jnp.maximum(b_last[ci] + m, max_lw[ci])
            m_next.append(m)
        m_state[h] = m
        m_prev = jnp.stack(m_prev, axis=0)
        m_next = jnp.stack(m_next, axis=0)
        inter = bb + m_prev
        m_t = jnp.maximum(inter, max_ld)
        d_mat = jnp.exp(log_d - m_t[:, :, :c])
        e_inter = jnp.exp(inter - m_t)
        sc = _bdot_nt(q3, k3) * d_mat
        scv = _bdot(sc, v3)
        rs = jnp.sum(sc, axis=-1, keepdims=True)
        kw = k3 * jnp.exp(log_w - m_next)
        kvn_scr[h * nc:(h + 1) * nc] = _bdot_tn(kw, jnp.concatenate([v3, ones_v], axis=2))
        keep = jnp.exp(b_last + m_prev - m_next)
        keep_scr[h * nc:(h + 1) * nc] = jnp.concatenate([keep, keep], axis=2)
        per_head.append((e_inter, scv, rs, jnp.exp(-m_t)))

    def step(ci, carry):
        rows = pl.ds(pl.multiple_of(ci * c, c), c)
        for h in range(N_HEADS):
            hs = slice(h * hd, (h + 1) * hd)
            idx = h * nc + ci
            cn = cn_state[h]
            qcn_scr[idx] = _dot(q_ref[0, rows, hs], cn)
            cn_state[h] = keep_scr[idx] * cn + kvn_scr[idx]
        return carry

    lax.fori_loop(0, nc, step, 0)
    for h in range(N_HEADS):
        hs = slice(h * hd, (h + 1) * hd)
        e_inter, scv, rs, e_mt = per_head[h]
        qcn = qcn_scr[h * nc:(h + 1) * nc]
        num = e_inter * qcn[:, :, :hd] + scv
        den = e_inter * qcn[:, :, hd:] + rs
        h_scr[:, hs] = (num / jnp.maximum(jnp.abs(den), e_mt)).reshape(ts, hd)
    y = _head_rms(h_scr[...], gn_ref[...]) * _sigmoid(og_ref[0].astype(F32))
    o_ref[0] = y.astype(o_ref.dtype)


def mlstm_mixer(proj, gates, i_bias_v, f_bias_v, gn_d, ts=512):
    b, s, _ = proj.shape
    w = GROUP_WIDTH
    hd = HEAD_DIM
    units = N_HEADS * (ts // CHUNK)

    def col(off):
        return pl.BlockSpec((1, ts, w), lambda bi, i, off=off: (bi, i, off))

    vec = pl.BlockSpec((1, GATE_LANES), lambda bi, i: (0, 0))
    big = lambda: pltpu.VMEM((ts, w), F32)
    return pl.pallas_call(
        _mlstm_kernel,
        grid=(b, s // ts),
        in_specs=[col(COL_DQ), col(COL_DK), col(COL_DV), col(COL_DO),
                  pl.BlockSpec((1, ts, GATE_LANES), lambda bi, i: (bi, i, 0)),
                  vec, vec, pl.BlockSpec((1, w), lambda bi, i: (0, 0))],
        out_specs=pl.BlockSpec((1, ts, w), lambda bi, i: (bi, i, 0)),
        out_shape=jax.ShapeDtypeStruct((b, s, w), BF16),
        scratch_shapes=[pltpu.VMEM((units, hd, 2 * hd), F32), pltpu.VMEM((units, 1, 2 * hd), F32),
                        pltpu.VMEM((units, CHUNK, 2 * hd), F32), big(),
                        pltpu.VMEM((N_HEADS, hd, 2 * hd), F32),
                        pltpu.VMEM((N_HEADS, 1, hd), F32)],
        compiler_params=_cparams(("arbitrary", "arbitrary")),
        name="mlstm",
    )(proj, proj, proj, proj, gates, i_bias_v, f_bias_v, gn_d)


SLAB = 16
MOE_TM = 256
META_ROWS = 8


def _route(h, wr, rb_col):
    tm = h.shape[0]
    logits = lax.dot_general(wr, h, (((1,), (1,)), ((), ())),
                             precision=HIGHEST, preferred_element_type=F32)
    scores = _sigmoid(logits)
    biased = scores + rb_col
    sr = [scores[e:e + 1, :] for e in range(N_EXPERTS)]
    br = [biased[e:e + 1, :] for e in range(N_EXPERTS)]
    n_groups = N_EXPERTS // EXPERTS_PER_GROUP
    group_scores = []
    for g in range(n_groups):
        a, b, cc, d = br[4 * g:4 * g + 4]
        p, q = jnp.maximum(a, b), jnp.minimum(a, b)
        r, s = jnp.maximum(cc, d), jnp.minimum(cc, d)
        group_scores.append(jnp.maximum(p, r) + jnp.maximum(jnp.minimum(p, r), jnp.maximum(q, s)))
    sel = jnp.zeros((1, tm), jnp.int32)
    best = group_scores[0]
    for g in range(1, n_groups):
        better = group_scores[g] > best
        sel = jnp.where(better, g, sel)
        best = jnp.where(better, group_scores[g], best)
    masked = [jnp.where(sel == (e // EXPERTS_PER_GROUP), br[e], -jnp.inf) for e in range(N_EXPERTS)]

    def top1(vals):
        idx = jnp.zeros((1, tm), jnp.int32)
        bv = vals[0]
        for e in range(1, N_EXPERTS):
            better = vals[e] > bv
            idx = jnp.where(better, e, idx)
            bv = jnp.where(better, vals[e], bv)
        return idx

    i1 = top1(masked)
    i2 = top1([jnp.where(i1 == e, -jnp.inf, masked[e]) for e in range(N_EXPERTS)])
    s1 = sum(jnp.where(i1 == e, sr[e], 0.0) for e in range(N_EXPERTS))
    s2 = sum(jnp.where(i2 == e, sr[e], 0.0) for e in range(N_EXPERTS))
    tot = s1 + s2
    return i1, i2, s1 / tot, s2 / tot


def _router_kernel(x_ref, sc_ref, sh_ref, nw_ref, wr_ref, rb_ref, hs_ref, meta_ref, wts_ref, cnt_ref,
                   run):
    @pl.when((pl.program_id(0) == 0) & (pl.program_id(1) == 0))
    def _():
        run[...] = jnp.zeros(run.shape, F32)

    h = _modulated_norm(x_ref[0], nw_ref[...], sc_ref[0], sh_ref[0])
    tm = h.shape[0]
    for s in range(SLAB):
        hs_ref[pl.ds(s, tm, stride=SLAB), :] = h[:, s * HEAD_DIM:(s + 1) * HEAD_DIM]
    i1, i2, w1, w2 = _route(h, wr_ref[...], rb_ref[...][:, 0:1])
    oh1 = jnp.concatenate([(i1 == e).astype(F32) for e in range(N_EXPERTS)], axis=0)
    oh2 = jnp.concatenate([(i2 == e).astype(F32) for e in range(N_EXPERTS)], axis=0)
    before = (lax.broadcasted_iota(jnp.int32, (tm, tm), 0)
              < lax.broadcasted_iota(jnp.int32, (tm, tm), 1)).astype(BF16)
    excl1 = jnp.dot(oh1.astype(BF16), before, preferred_element_type=F32)
    excl2 = jnp.dot(oh2.astype(BF16), before, preferred_element_type=F32)
    tot1 = jnp.sum(oh1, axis=1, keepdims=True)
    tot2 = jnp.sum(oh2, axis=1, keepdims=True)
    base = run[...][:, 0:1]
    rank1 = jnp.sum(oh1 * (base + excl1), axis=0, keepdims=True)
    rank2 = jnp.sum(oh2 * (base + tot1 + excl2), axis=0, keepdims=True)
    run[...] = run[...] + (tot1 + tot2)
    cnt_ref[...] = run[...]
    meta_ref[...] = jnp.concatenate(
        [i1, i2, rank1.astype(jnp.int32), rank2.astype(jnp.int32),
         jnp.zeros((META_ROWS - 4, tm), jnp.int32)], axis=0)
    wts_ref[...] = jnp.concatenate([w1, w2, jnp.zeros((META_ROWS - 2, tm), F32)], axis=0)


def router(x, scale, shift, norm_w, w_router_t, router_bias_col, tm=512):
    b, s, d = x.shape
    nt = s // tm
    t = b * s
    return pl.pallas_call(
        _router_kernel,
        grid=(b, nt),
        in_specs=[pl.BlockSpec((1, tm, d), lambda bi, i: (bi, i, 0)),
                  pl.BlockSpec((1, 1, d), lambda bi, i: (bi, 0, 0)),
                  pl.BlockSpec((1, 1, d), lambda bi, i: (bi, 0, 0)),
                  pl.BlockSpec((1, d), lambda bi, i: (0, 0)),
                  pl.BlockSpec((N_EXPERTS, d), lambda bi, i: (0, 0)),
                  pl.BlockSpec((N_EXPERTS, GATE_LANES), lambda bi, i: (0, 0))],
        out_specs=[pl.BlockSpec((tm * SLAB, HEAD_DIM), lambda bi, i: (bi * nt + i, 0)),
                   pl.BlockSpec((META_ROWS, tm), lambda bi, i: (0, bi * nt + i)),
                   pl.BlockSpec((META_ROWS, tm), lambda bi, i: (0, bi * nt + i)),
                   pl.BlockSpec((N_EXPERTS, GATE_LANES), lambda bi, i: (0, 0))],
        out_shape=[jax.ShapeDtypeStruct((t * SLAB, HEAD_DIM), F32),
                   jax.ShapeDtypeStruct((META_ROWS, t), jnp.int32),
                   jax.ShapeDtypeStruct((META_ROWS, t), F32),
                   jax.ShapeDtypeStruct((N_EXPERTS, GATE_LANES), F32)],
        scratch_shapes=[pltpu.VMEM((N_EXPERTS, GATE_LANES), F32)],
        compiler_params=_cparams(("arbitrary", "arbitrary")),
        name="router",
    )(x, scale, shift, norm_w, w_router_t, router_bias_col)


def routing_tables(meta, counts, tm=MOE_TM):
    t = meta.shape[1]
    cnt = counts[:, 0].astype(jnp.int32)
    padded = (cnt + tm - 1) // tm * tm
    ends = jnp.cumsum(padded)
    off = ends - padded
    eid = meta[0:2]
    pair_off = sum(jnp.where(eid == e, off[e], 0) for e in range(N_EXPERTS))
    pos = pair_off + meta[2:4]
    n_tiles = 2 * t // tm + N_EXPERTS
    tile_start = jnp.arange(n_tiles, dtype=jnp.int32) * tm
    tile_expert = jnp.minimum(jnp.sum((ends[None, :] <= tile_start[:, None]).astype(jnp.int32), axis=1),
                              N_EXPERTS - 1).astype(jnp.int32)
    n_valid = (ends[-1] // tm).astype(jnp.int32).reshape(1)
    return pos.reshape(-1).astype(jnp.int32), tile_expert, n_valid


def _slab_rows(i):
    return pl.ds(pl.multiple_of(i * SLAB, SLAB), SLAB)


def _dispatch_kernel(pos_ref, hs_ref, xz_ref, xs_ref, pair_ref, sem):
    del xz_ref
    tmd = hs_ref.shape[0] // SLAB
    t = pos_ref.shape[0] // 2
    base = pl.program_id(0) * tmd

    @pl.when(pl.program_id(0) == 0)
    def _():
        def fill(r, carry):
            pair_ref[r] = -1
            return carry

        lax.fori_loop(0, pair_ref.shape[0], fill, 0, unroll=8)

    def issue(r, carry):
        for k in range(2):
            p = pos_ref[k * t + base + r]
            pair_ref[p] = k * t + base + r
            pltpu.make_async_copy(hs_ref.at[_slab_rows(r), :], xs_ref.at[_slab_rows(p), :], sem).start()
        return carry

    lax.fori_loop(0, tmd, issue, 0, unroll=8)
    for k in range(2):
        pltpu.make_async_copy(hs_ref, xs_ref.at[pl.ds(0, tmd * SLAB), :], sem).wait()


def dispatch(pos, hs, xs_init, tmd=512):
    t = hs.shape[0] // SLAB
    return pl.pallas_call(
        _dispatch_kernel,
        grid_spec=pltpu.PrefetchScalarGridSpec(
            num_scalar_prefetch=1,
            grid=(t // tmd,),
            in_specs=[pl.BlockSpec((tmd * SLAB, HEAD_DIM), lambda i, pos: (i, 0)),
                      pl.BlockSpec(memory_space=pl.ANY)],
            out_specs=[pl.BlockSpec(memory_space=pl.ANY), pl.BlockSpec(memory_space=pltpu.SMEM)],
            scratch_shapes=[pltpu.SemaphoreType.DMA]),
        out_shape=[jax.ShapeDtypeStruct(xs_init.shape, F32),
                   jax.ShapeDtypeStruct((xs_init.shape[0] // SLAB,), jnp.int32)],
        input_output_aliases={2: 0},
        compiler_params=_cparams(("arbitrary",)),
        name="moe_dispatch",
    )(pos, hs, xs_init)


def _experts_kernel(te_ref, nv_ref, pair_ref, xs_ref, wg_ref, wu_ref, wd_ref, yi_ref, yt_ref,
                    wgb, wub, wdb, yblk, sem):
    del yi_ref
    j = pl.program_id(0)
    nv = nv_ref[0]
    tm = xs_ref.shape[0] // SLAB

    def wait_rows():
        pltpu.make_async_copy(yblk, yt_ref.at[pl.ds(0, tm * SLAB), :], sem).wait()

    @pl.when(j < nv)
    def _():
        prev = te_ref[jnp.maximum(j - 1, 0)]

        @pl.when((j == 0) | (prev != te_ref[j]))
        def _():
            wgb[...] = wg_ref[0, 0].astype(BF16)
            wub[...] = wu_ref[0, 0].astype(BF16)
            wdb[...] = wd_ref[0, 0].astype(BF16)

        x = jnp.concatenate([xs_ref[pl.ds(s, tm, stride=SLAB), :] for s in range(SLAB)],
                            axis=1).astype(BF16)
        hg = jnp.dot(x, wgb[...], preferred_element_type=F32)
        hu = jnp.dot(x, wub[...], preferred_element_type=F32)
        y = _dot(_silu(hg) * hu, wdb[...])

        @pl.when(j > 0)
        def _():
            wait_rows()

        for s in range(SLAB):
            yblk[pl.ds(s, tm, stride=SLAB), :] = y[:, s * HEAD_DIM:(s + 1) * HEAD_DIM]

        spare = yt_ref.shape[0] // SLAB - N_EXPERTS * tm + te_ref[j] * tm

        def issue(r, carry):
            pair = pair_ref[j * tm + r]
            d = jnp.where(pair < 0, spare + r, pair)
            pltpu.make_async_copy(yblk.at[_slab_rows(r), :], yt_ref.at[_slab_rows(d), :], sem).start()
            return carry

        lax.fori_loop(0, tm, issue, 0, unroll=8)

        @pl.when(j == nv - 1)
        def _():
            wait_rows()


def grouped_experts(tile_expert, n_valid, row_pair, xs, w_gate, w_up, w_down, layer, yt_init,
                    tm=MOE_TM):
    _, ne, d, dff = w_gate.shape
    n_tiles = tile_expert.shape[0]

    def row_tile(j, te, nv, pair):
        return (jnp.minimum(j, nv[0] - 1), 0)

    def expert(j, te, nv, pair):
        return (layer, te[jnp.minimum(j, nv[0] - 1)], 0, 0)

    return pl.pallas_call(
        _experts_kernel,
        grid_spec=pltpu.PrefetchScalarGridSpec(
            num_scalar_prefetch=3,
            grid=(n_tiles,),
            in_specs=[pl.BlockSpec((tm * SLAB, HEAD_DIM), row_tile),
                      pl.BlockSpec((1, 1, d, dff), expert),
                      pl.BlockSpec((1, 1, d, dff), expert),
                      pl.BlockSpec((1, 1, dff, d), expert),
                      pl.BlockSpec(memory_space=pl.ANY)],
            out_specs=pl.BlockSpec(memory_space=pl.ANY),
            scratch_shapes=[pltpu.VMEM((d, dff), BF16), pltpu.VMEM((d, dff), BF16),
                            pltpu.VMEM((dff, d), BF16), pltpu.VMEM((tm * SLAB, HEAD_DIM), F32),
                            pltpu.SemaphoreType.DMA]),
        out_shape=jax.ShapeDtypeStruct(yt_init.shape, F32),
        input_output_aliases={7: 0},
        compiler_params=_cparams(("arbitrary",)),
        name="moe_experts",
    )(tile_expert, n_valid, row_pair, xs, w_gate, w_up, w_down, yt_init)


def _combine_kernel(x_ref, y1_ref, y2_ref, w_ref, gate_ref, o_ref):
    tm = x_ref.shape[1]
    w = w_ref[...]
    w1 = w[:, 0:1]
    w2 = w[:, 1:2]
    parts = [w1 * y1_ref[pl.ds(s, tm, stride=SLAB), :] + w2 * y2_ref[pl.ds(s, tm, stride=SLAB), :]
             for s in range(SLAB)]
    o_ref[0] = x_ref[0] + gate_ref[0] * jnp.concatenate(parts, axis=1)


def combine(x, yt, w_cols, gate, tm=256):
    b, s, d = x.shape
    nt = s // tm
    t = b * s
    return pl.pallas_call(
        _combine_kernel,
        grid=(b, nt),
        in_specs=[pl.BlockSpec((1, tm, d), lambda bi, i: (bi, i, 0)),
                  pl.BlockSpec((tm * SLAB, HEAD_DIM), lambda bi, i: (bi * nt + i, 0)),
                  pl.BlockSpec((tm * SLAB, HEAD_DIM), lambda bi, i: (t // tm + bi * nt + i, 0)),
                  pl.BlockSpec((tm, META_ROWS), lambda bi, i: (bi * nt + i, 0)),
                  pl.BlockSpec((1, 1, d), lambda bi, i: (bi, 0, 0))],
        out_specs=pl.BlockSpec((1, tm, d), lambda bi, i: (bi, i, 0)),
        out_shape=jax.ShapeDtypeStruct((b, s, d), F32),
        compiler_params=_cparams(("arbitrary", "arbitrary")),
        name="moe_combine",
    )(x, yt, yt, w_cols, gate)


def _final_norm_kernel(x_ref, nw_ref, o_ref):
    x = x_ref[0]
    ms = jnp.mean(x * x, axis=-1, keepdims=True)
    o_ref[0] = x * lax.rsqrt(ms + EPS) * nw_ref[...]


def final_norm(x, norm_w, tm=1024):
    b, s, d = x.shape
    return pl.pallas_call(
        _final_norm_kernel,
        grid=(b, s // tm),
        in_specs=[pl.BlockSpec((1, tm, d), lambda bi, i: (bi, i, 0)),
                  pl.BlockSpec((1, d), lambda bi, i: (0, 0))],
        out_specs=pl.BlockSpec((1, tm, d), lambda bi, i: (bi, i, 0)),
        out_shape=jax.ShapeDtypeStruct((b, s, d), F32),
        compiler_params=_cparams(("arbitrary", "arbitrary")),
        name="final_norm",
    )(x, norm_w)


def _split_w_in(w_in):
    gw = GROUP_WIDTH
    a_end = 3 * gw
    b_end = a_end + 2 * gw
    c_main_end = b_end + 4 * gw
    c_end = c_main_end + 2 * N_HEADS
    d_main_end = c_end + 4 * gw
    main = jnp.concatenate([w_in[:, :c_main_end], w_in[:, c_end:d_main_end]], axis=1)
    small = jnp.concatenate([w_in[:, c_main_end:c_end], w_in[:, d_main_end:]], axis=1)
    small = jnp.pad(small, ((0, 0), (0, GATE_LANES - small.shape[1])))
    return main.astype(BF16), small.astype(BF16)


def _lanes(vec, offset):
    return jnp.zeros((1, GATE_LANES), F32).at[0, offset:offset + N_HEADS].set(vec)


def _rope_tables(s):
    half = HEAD_DIM // 2
    inv_freq = ROPE_THETA ** (-jnp.arange(half, dtype=F32) / half)
    ang = jnp.arange(s, dtype=F32)[:, None] * inv_freq[None, :]
    cos, sin = jnp.cos(ang), jnp.sin(ang)
    return jnp.concatenate([cos, cos], axis=1), jnp.concatenate([-sin, sin], axis=1)


def hybrid_mixer_layer(x, sc1, sh1, g1, norm_w, w_in, w_out, gn_a, conv_b_w, conv_b_b, w_rg, b_rg,
                       w_ig, b_ig, lru_lambda, gn_b, conv_c_w, gdn_a_log, gdn_dt_bias, gdn_norm,
                       mlstm_i_bias, mlstm_f_bias, gn_d, rope):
    w_main, w_small = _split_w_in(w_in)
    proj, gates = in_projection(x, sc1, sh1, norm_w.reshape(1, -1), w_main, w_small)
    row = lambda v: v.reshape(1, -1)
    y_a = dilated_attention_mixer(proj, rope[0], rope[1], row(gn_a))
    y_b = rg_lru_mixer(proj, conv_b_w, row(conv_b_b), w_rg.astype(BF16), row(b_rg),
                       w_ig.astype(BF16), row(b_ig), row(lru_lambda), row(gn_b))
    y_c = gated_delta_mixer(proj, gates, conv_c_w, _lanes(gdn_a_log, N_HEADS),
                            _lanes(gdn_dt_bias, N_HEADS), row(jnp.tile(gdn_norm, N_HEADS)))
    y_d = mlstm_mixer(proj, gates, _lanes(mlstm_i_bias, 2 * N_HEADS),
                      _lanes(mlstm_f_bias, 3 * N_HEADS), row(gn_d))
    return out_projection(x, (y_a, y_b, y_c, y_d), w_out.astype(BF16), g1)


def moe_layer(x, sc2, sh2, g2, norm_w, w_router, router_bias, w_gate, w_up, w_down, layer, xs_buf,
              yt_buf):
    rb = jnp.broadcast_to(router_bias.reshape(-1, 1), (N_EXPERTS, GATE_LANES))
    hs, meta, wts, counts = router(x, sc2, sh2, norm_w.reshape(1, -1), w_router.T, rb)
    pos, tile_expert, n_valid = routing_tables(meta, counts)
    xs, row_pair = dispatch(pos, hs, xs_buf)
    yt = grouped_experts(tile_expert, n_valid, row_pair, xs, w_gate, w_up, w_down, layer, yt_buf)
    return combine(x, yt, wts.T, g2), xs, yt


def moe_buffers(n_tokens):
    rows = 2 * n_tokens + N_EXPERTS * MOE_TM
    return jnp.zeros((rows * SLAB, HEAD_DIM), F32), jnp.zeros((rows * SLAB, HEAD_DIM), F32)


def kernel(x, c, norm_mix, norm_ffn, norm_final, w_ada, b_ada, w_in, w_out, gn_a, conv_b_w, conv_b_b, w_rg, b_rg, w_ig, b_ig, lru_lambda, gn_b, conv_c_w, gdn_a_log, gdn_dt_bias, gdn_norm, mlstm_i_bias, mlstm_f_bias, gn_d, w_router, router_bias, moe_w_gate, moe_w_up, moe_w_down):
    depth = w_ada.shape[0]
    b, s, d = x.shape
    mod = ada_modulation(c, w_ada, b_ada)
    rope = _rope_tables(s)
    xs_buf, yt_buf = moe_buffers(b * s)
    for l in range(depth):
        sh1, sc1, g1, sh2, sc2, g2 = [mod[l, :, None, i * d:(i + 1) * d] for i in range(6)]
        x = hybrid_mixer_layer(x, sc1, sh1, g1, norm_mix[l], w_in[l], w_out[l], gn_a[l],
                               conv_b_w[l], conv_b_b[l], w_rg[l], b_rg[l], w_ig[l], b_ig[l],
                               lru_lambda[l], gn_b[l], conv_c_w[l], gdn_a_log[l], gdn_dt_bias[l],
                               gdn_norm[l], mlstm_i_bias[l], mlstm_f_bias[l], gn_d[l], rope)
        x, xs_buf, yt_buf = moe_layer(x, sc2, sh2, g2, norm_ffn[l], w_router, router_bias,
                                      moe_w_gate, moe_w_up, moe_w_down, l, xs_buf, yt_buf)
    return final_norm(x, norm_final.reshape(1, -1))
```

```python
import functools
import math

import jax
import jax.numpy as jnp
from jax import lax
from jax.experimental import pallas as pl
from jax.experimental.pallas import tpu as pltpu

F32 = jnp.float32
BF16 = jnp.bfloat16
HIGHEST = lax.Precision.HIGHEST

HEAD_DIM = 128
N_HEADS = 4
GROUP_WIDTH = N_HEADS * HEAD_DIM
DILATED_PAIRS = ((128, 1), (512, 4), (2048, 16))
ATTN_BLOCK = 128
ROPE_THETA = 10000.0
CONV_WIDTH = 4
LRU_C = 8.0
CHUNK = 64
N_EXPERTS = 16
EXPERTS_PER_GROUP = 4
EPS = 1e-6
M_INIT = -1e30
NEG = -1e30
SUBLANES = 8
VMEM_LIMIT = 56 * 1024 * 1024

COL_AQ, COL_AK, COL_AV = 0, 4, 8
COL_BX, COL_BG = 3, 4
COL_CQ, COL_CK, COL_CV, COL_CZ = 5, 6, 7, 8
COL_DQ, COL_DK, COL_DV, COL_DO = 9, 10, 11, 12
MAIN_COLS = 13 * GROUP_WIDTH
GATE_LANES = 128


def _cparams(sem):
    return pltpu.CompilerParams(dimension_semantics=sem, vmem_limit_bytes=VMEM_LIMIT)


def _dot(a, b):
    return jnp.dot(a.astype(BF16), b.astype(BF16), preferred_element_type=F32)


def _dot_nt(a, b):
    return lax.dot_general(a.astype(BF16), b.astype(BF16), (((1,), (1,)), ((), ())),
                           preferred_element_type=F32)


def _dot_tn(a, b):
    return lax.dot_general(a.astype(BF16), b.astype(BF16), (((0,), (0,)), ((), ())),
                           preferred_element_type=F32)


def _dot_exact(a, b):
    return jnp.dot(a, b, precision=HIGHEST, preferred_element_type=F32)


def _row_form(col_b):
    c = col_b.shape[0]
    lane = lax.broadcasted_iota(jnp.int32, col_b.shape, 1)
    picked = jnp.where(lane == 0, col_b, 0.0)
    return lax.dot_general(jnp.ones((c, col_b.shape[1]), F32), picked, (((1,), (1,)), ((), ())),
                           precision=HIGHEST, preferred_element_type=F32)


def _softplus(z):
    return jnp.maximum(z, 0.0) + jnp.log1p(jnp.exp(-jnp.abs(z)))


def _sigmoid(z):
    return 1.0 / (1.0 + jnp.exp(-z))


def _silu(z):
    return z * _sigmoid(z)


def _head_rms(x, gain):
    outs = []
    for g in range(x.shape[1] // HEAD_DIM):
        xg = x[:, g * HEAD_DIM:(g + 1) * HEAD_DIM]
        ms = jnp.mean(xg * xg, axis=-1, keepdims=True)
        outs.append(xg * lax.rsqrt(ms + EPS))
    y = outs[0] if len(outs) == 1 else jnp.concatenate(outs, axis=1)
    return y * gain


def _tril(c, strict=False):
    row = lax.broadcasted_iota(jnp.int32, (c, c), 0)
    col = lax.broadcasted_iota(jnp.int32, (c, c), 1)
    return (col < row) if strict else (col <= row)


def _ada_kernel(c_ref, w_ref, b_ref, o_ref):
    c = c_ref[...]
    o_ref[0] = _dot(_silu(c), w_ref[0]) + b_ref[0]


def ada_modulation(c, w_ada, b_ada):
    depth, d, n = w_ada.shape
    b = c.shape[0]
    rows = -(-b // SUBLANES) * SUBLANES
    c_pad = jnp.zeros((rows, d), F32).at[:b].set(c)
    tn = 1024
    out = pl.pallas_call(
        _ada_kernel,
        grid=(depth, n // tn),
        in_specs=[pl.BlockSpec((rows, d), lambda l, j: (0, 0)),
                  pl.BlockSpec((1, d, tn), lambda l, j: (l, 0, j)),
                  pl.BlockSpec((1, 1, tn), lambda l, j: (l, 0, j))],
        out_specs=pl.BlockSpec((1, rows, tn), lambda l, j: (l, 0, j)),
        out_shape=jax.ShapeDtypeStruct((depth, rows, n), F32),
        compiler_params=_cparams(("arbitrary", "arbitrary")),
        name="ada_modulation",
    )(c_pad, w_ada, b_ada.reshape(depth, 1, n))
    return out[:, :b]


def _modulated_norm(x, nw, sc, sh):
    ms = jnp.mean(x * x, axis=-1, keepdims=True)
    return (x * lax.rsqrt(ms + EPS) * nw) * (1.0 + sc) + sh


def _in_proj_kernel(x_ref, sc_ref, sh_ref, nw_ref, w_ref, wg_ref, proj_ref, gates_ref, h_scr):
    @pl.when(pl.program_id(2) == 0)
    def _():
        h = _modulated_norm(x_ref[0], nw_ref[...], sc_ref[0], sh_ref[0]).astype(BF16)
        h_scr[...] = h
        gates_ref[0] = jnp.dot(h, wg_ref[...], preferred_element_type=F32)

    proj_ref[0] = jnp.dot(h_scr[...], w_ref[...], preferred_element_type=F32).astype(BF16)


def in_projection(x, scale, shift, norm_w, w_main, w_gate, tm=1024, tn=512):
    b, s, d = x.shape
    n = w_main.shape[1]
    return pl.pallas_call(
        _in_proj_kernel,
        grid=(b, s // tm, n // tn),
        in_specs=[pl.BlockSpec((1, tm, d), lambda bi, i, j: (bi, i, 0)),
                  pl.BlockSpec((1, 1, d), lambda bi, i, j: (bi, 0, 0)),
                  pl.BlockSpec((1, 1, d), lambda bi, i, j: (bi, 0, 0)),
                  pl.BlockSpec((1, d), lambda bi, i, j: (0, 0)),
                  pl.BlockSpec((d, tn), lambda bi, i, j: (0, j)),
                  pl.BlockSpec((d, GATE_LANES), lambda bi, i, j: (0, 0))],
        out_specs=[pl.BlockSpec((1, tm, tn), lambda bi, i, j: (bi, i, j)),
                   pl.BlockSpec((1, tm, GATE_LANES), lambda bi, i, j: (bi, i, 0))],
        out_shape=[jax.ShapeDtypeStruct((b, s, n), BF16),
                   jax.ShapeDtypeStruct((b, s, GATE_LANES), F32)],
        scratch_shapes=[pltpu.VMEM((tm, d), BF16)],
        compiler_params=_cparams(("arbitrary", "arbitrary", "arbitrary")),
        name="in_projection",
    )(x, scale, shift, norm_w, w_main, w_gate)


def _out_proj_kernel(x_ref, ya_ref, yb_ref, yc_ref, yd_ref, w_ref, g_ref, o_ref):
    acc = None
    for k, y_ref in enumerate((ya_ref, yb_ref, yc_ref, yd_ref)):
        part = jnp.dot(y_ref[0], w_ref[k * GROUP_WIDTH:(k + 1) * GROUP_WIDTH, :],
                       preferred_element_type=F32)
        acc = part if acc is None else acc + part
    o_ref[0] = x_ref[0] + g_ref[0] * acc


def out_projection(x, ys, w_out, gate, tm=512):
    b, s, d = x.shape
    yspec = pl.BlockSpec((1, tm, GROUP_WIDTH), lambda bi, i: (bi, i, 0))
    return pl.pallas_call(
        _out_proj_kernel,
        grid=(b, s // tm),
        in_specs=[pl.BlockSpec((1, tm, d), lambda bi, i: (bi, i, 0)),
                  yspec, yspec, yspec, yspec,
                  pl.BlockSpec(w_out.shape, lambda bi, i: (0, 0)),
                  pl.BlockSpec((1, 1, d), lambda bi, i: (bi, 0, 0))],
        out_specs=pl.BlockSpec((1, tm, d), lambda bi, i: (bi, i, 0)),
        out_shape=jax.ShapeDtypeStruct((b, s, d), F32),
        compiler_params=_cparams(("arbitrary", "arbitrary")),
        name="out_projection",
    )(x, *ys, w_out, gate)


def _attn_kernel(q_ref, k_ref, v_ref, cos_ref, sin_ref, gn_ref, o_ref, qs, ks, vs, acc, ms, ls,
                 q4, k4, v4):
    s = q_ref.shape[1]
    blk = ATTN_BLOCK
    cos = cos_ref[...]
    sin = sin_ref[...]

    def rot(t):
        return t * cos + pltpu.roll(t, HEAD_DIM // 2, 1) * sin

    qs[...] = rot(q_ref[0].astype(F32)) * (1.0 / math.sqrt(HEAD_DIM))
    ks[...] = rot(k_ref[0].astype(F32))
    vs[...] = v_ref[0].astype(F32)

    row = lax.broadcasted_iota(jnp.int32, (blk, 2 * blk), 0)
    col = lax.broadcasted_iota(jnp.int32, (blk, 2 * blk), 1)
    own_ok = (col >= blk) & (col - blk <= row)

    def run_branch(refs, stride, span, locate, first_branch):
        q_s, k_s, v_s, acc_s, m_s, l_s = refs

        def rows(start):
            if stride == 1:
                return pl.ds(pl.multiple_of(start, blk), blk)
            return pl.ds(start, blk, stride=stride)

        def body(idx, carry):
            start, has_prev = locate(idx)
            pstart = jnp.where(has_prev, start - span, start)
            qb = q_s[rows(start), :]
            kc = jnp.concatenate([k_s[rows(pstart), :], k_s[rows(start), :]], axis=0)
            vc = jnp.concatenate([v_s[rows(pstart), :], v_s[rows(start), :]], axis=0)
            sc = _dot_nt(qb, kc)
            first = jnp.where(has_prev, 0, 4 * blk)
            valid = own_ok | ((col < blk) & (col >= row + first))
            sc = jnp.where(valid, sc, NEG)
            mb = jnp.max(sc, axis=-1, keepdims=True)
            p = jnp.exp(sc - mb)
            den = jnp.sum(p, axis=-1, keepdims=True)
            num = _dot(p, vc)
            mb = jnp.broadcast_to(mb, (blk, HEAD_DIM))
            den = jnp.broadcast_to(den, (blk, HEAD_DIM))
            if first_branch:
                acc_s[rows(start), :] = num
                m_s[rows(start), :] = mb
                l_s[rows(start), :] = den
            else:
                m_old = m_s[rows(start), :]
                m_new = jnp.maximum(m_old, mb)
                a_old = jnp.exp(m_old - m_new)
                a_blk = jnp.exp(mb - m_new)
                acc_s[rows(start), :] = acc_s[rows(start), :] * a_old + num * a_blk
                l_s[rows(start), :] = l_s[rows(start), :] * a_old + den * a_blk
                m_s[rows(start), :] = m_new
            return carry

        lax.fori_loop(0, s // blk, body, 0, unroll=8)

    def natural(dil):
        nb = s // (blk * dil)

        def locate(idx):
            n = idx % nb
            return idx // nb + n * (blk * dil), n > 0

        return locate

    (_, d0), (_, d1), (_, d2) = DILATED_PAIRS
    run_branch((qs, ks, vs, acc, ms, ls), d0, blk * d0, natural(d0), True)
    run_branch((qs, ks, vs, acc, ms, ls), d1, blk * d1, natural(d1), False)

    fold = 4
    seg = s // fold
    sub = d2 // fold

    def deinterleave(src, dst):
        for r1 in range(fold):
            dst[r1 * seg:(r1 + 1) * seg, :] = src[pl.ds(r1, seg, stride=fold), :]

    deinterleave(qs, q4)
    deinterleave(ks, k4)
    deinterleave(vs, v4)
    deinterleave(acc, qs)
    deinterleave(ms, ks)
    deinterleave(ls, vs)
    nb2 = s // (blk * d2)

    def folded(idx):
        r = idx // nb2
        n = idx % nb2
        return (r % fold) * seg + r // fold + n * (blk * sub), n > 0

    run_branch((q4, k4, v4, qs, ks, vs), sub, blk * sub, folded, False)
    y4 = _head_rms(qs[...] / vs[...], gn_ref[...])
    for r1 in range(fold):
        acc[pl.ds(r1, seg, stride=fold), :] = y4[r1 * seg:(r1 + 1) * seg, :]
    o_ref[0] = acc[...].astype(o_ref.dtype)


def dilated_attention_mixer(proj, cos2, sin2, gn_a):
    b, s, _ = proj.shape
    hd = HEAD_DIM

    def col(off):
        return pl.BlockSpec((1, s, hd), lambda bi, h, off=off: (bi, 0, off + h))

    return pl.pallas_call(
        _attn_kernel,
        grid=(b, N_HEADS),
        in_specs=[col(COL_AQ), col(COL_AK), col(COL_AV),
                  pl.BlockSpec((s, hd), lambda bi, h: (0, 0)),
                  pl.BlockSpec((s, hd), lambda bi, h: (0, 0)),
                  pl.BlockSpec((1, hd), lambda bi, h: (0, h))],
        out_specs=pl.BlockSpec((1, s, hd), lambda bi, h: (bi, 0, h)),
        out_shape=jax.ShapeDtypeStruct((b, s, GROUP_WIDTH), BF16),
        scratch_shapes=[pltpu.VMEM((s, hd), F32) for _ in range(9)],
        compiler_params=_cparams(("arbitrary", "arbitrary")),
        name="dilated_attention",
    )(proj, proj, proj, cos2, sin2, gn_a)


def _load_conv_tile(x_ref, xpad, ts):
    @pl.when(pl.program_id(1) == 0)
    def _():
        xpad[0:SUBLANES, :] = jnp.zeros((SUBLANES, xpad.shape[1]), F32)

    @pl.when(pl.program_id(1) > 0)
    def _():
        xpad[0:SUBLANES, :] = xpad[ts:ts + SUBLANES, :]

    xpad[SUBLANES:SUBLANES + ts, :] = x_ref[0].astype(F32)


def _conv_from_pad(xpad, w, ts):
    y = None
    for j in range(CONV_WIDTH):
        off = SUBLANES - (CONV_WIDTH - 1) + j
        term = xpad[off:off + ts, :] * w[j:j + 1, :]
        y = term if y is None else y + term
    return y


def _lru_kernel(x_ref, g_ref, cw_ref, cb_ref, wr_ref, br_ref, wi_ref, bi_ref, lam_ref, gn_ref,
                o_ref, xpad, a_scr, u_scr, carry):
    ts = x_ref.shape[1]
    w = x_ref.shape[2]
    _load_conv_tile(x_ref, xpad, ts)

    @pl.when(pl.program_id(1) == 0)
    def _():
        carry[...] = jnp.zeros(carry.shape, F32)

    xc = _conv_from_pad(xpad, cw_ref[...], ts) + cb_ref[...]
    r_parts, i_parts = [], []
    for g in range(N_HEADS):
        xg = xc[:, g * HEAD_DIM:(g + 1) * HEAD_DIM]
        r_parts.append(_dot(xg, wr_ref[g]))
        i_parts.append(_dot(xg, wi_ref[g]))
    r = _sigmoid(jnp.concatenate(r_parts, axis=1) + br_ref[...])
    ig = _sigmoid(jnp.concatenate(i_parts, axis=1) + bi_ref[...])
    log_a = (-LRU_C) * r * _softplus(-lam_ref[...])
    a_scr[...] = jnp.exp(log_a)
    u_scr[...] = jnp.sqrt(1.0 - jnp.exp(2.0 * log_a)) * (ig * xc)

    sub = lax.broadcasted_iota(jnp.int32, (SUBLANES, w), 0)

    def body(j, h_prev):
        rows = pl.ds(pl.multiple_of(j * SUBLANES, SUBLANES), SUBLANES)
        a = a_scr[rows, :]
        u = u_scr[rows, :]
        for sh in (1, 2, 4):
            keep = sub >= sh
            a_sh = jnp.where(keep, pltpu.roll(a, sh, 0), 1.0)
            u_sh = jnp.where(keep, pltpu.roll(u, sh, 0), 0.0)
            u = a * u_sh + u
            a = a * a_sh
        h = u + a * h_prev
        u_scr[rows, :] = h
        return jnp.broadcast_to(h[SUBLANES - 1:SUBLANES, :], (SUBLANES, w))

    carry[...] = lax.fori_loop(0, ts // SUBLANES, body, carry[...])
    y = _head_rms(u_scr[...], gn_ref[...]) * jax.nn.gelu(g_ref[0].astype(F32))
    o_ref[0] = y.astype(o_ref.dtype)


def rg_lru_mixer(proj, conv_w, conv_b, w_rg, b_rg, w_ig, b_ig, lam, gn_b, ts=1024):
    b, s, _ = proj.shape
    w = GROUP_WIDTH
    vec = pl.BlockSpec((1, w), lambda bi, i: (0, 0))
    blockdiag = pl.BlockSpec((N_HEADS, HEAD_DIM, HEAD_DIM), lambda bi, i: (0, 0, 0))
    return pl.pallas_call(
        _lru_kernel,
        grid=(b, s // ts),
        in_specs=[pl.BlockSpec((1, ts, w), lambda bi, i: (bi, i, COL_BX)),
                  pl.BlockSpec((1, ts, w), lambda bi, i: (bi, i, COL_BG)),
                  pl.BlockSpec((CONV_WIDTH, w), lambda bi, i: (0, 0)),
                  vec, blockdiag, vec, blockdiag, vec, vec, vec],
        out_specs=pl.BlockSpec((1, ts, w), lambda bi, i: (bi, i, 0)),
        out_shape=jax.ShapeDtypeStruct((b, s, w), BF16),
        scratch_shapes=[pltpu.VMEM((ts + SUBLANES, w), F32), pltpu.VMEM((ts, w), F32),
                        pltpu.VMEM((ts, w), F32), pltpu.VMEM((SUBLANES, w), F32)],
        compiler_params=_cparams(("arbitrary", "arbitrary")),
        name="rg_lru",
    )(proj, proj, conv_w, conv_b, w_rg, b_rg, w_ig, b_ig, lam, gn_b)


def _lane_pick(x, lane_idx):
    lane = lax.broadcasted_iota(jnp.int32, x.shape, 1)
    col = jnp.sum(jnp.where(lane == lane_idx, x, 0.0), axis=-1, keepdims=True)
    return jnp.broadcast_to(col, x.shape)


def _bdot(a, b):
    return jnp.einsum("nij,njk->nik", a.astype(BF16), b.astype(BF16), preferred_element_type=F32)


def _bdot_nt(a, b):
    return jnp.einsum("nid,njd->nij", a.astype(BF16), b.astype(BF16), preferred_element_type=F32)


def _bdot_tn(a, b):
    return jnp.einsum("ncd,nce->nde", a.astype(BF16), b.astype(BF16), preferred_element_type=F32)


def _chunk_cumsum(x, c):
    nc = x.shape[0] // c
    tri = jnp.broadcast_to(_tril(c).astype(F32), (nc, c, c))
    return jnp.einsum("nij,njk->nik", tri, x.reshape(nc, c, x.shape[1]), precision=HIGHEST,
                      preferred_element_type=F32)


def _row_forms(x3, specs):
    nc, c, w = x3.shape
    rows = len(specs) * c
    lane = lax.broadcasted_iota(jnp.int32, (rows, w), 1)
    row = lax.broadcasted_iota(jnp.int32, (rows, w), 0)
    sel = jnp.zeros((rows, w), F32)
    for i, spec in enumerate(specs):
        in_rows = (row >= i * c) & (row < (i + 1) * c)
        for ln, coeff in spec:
            sel = jnp.where(in_rows & (lane == ln), coeff, sel)
    out = jnp.einsum("nil,njl->nij", jnp.broadcast_to(sel, (nc, rows, w)), x3, precision=HIGHEST,
                     preferred_element_type=F32)
    return [out[:, i * c:(i + 1) * c, :] for i in range(len(specs))]


def _lane_pick3(x3, lane_idx):
    lane = lax.broadcasted_iota(jnp.int32, x3.shape, 2)
    col = jnp.sum(jnp.where(lane == lane_idx, x3, 0.0), axis=-1, keepdims=True)
    return jnp.broadcast_to(col, x3.shape[:2] + (HEAD_DIM,))


def _gdn_kernel(q_ref, k_ref, v_ref, z_ref, gt_ref, cwq_ref, cwk_ref, cwv_ref, alog_ref, dtb_ref,
                gn_ref, o_ref, qpad, kpad, vpad, p_scr, n_scr, qp_scr, op_scr, gl_scr, o_scr, state):
    ts = q_ref.shape[1]
    c = CHUNK
    hd = HEAD_DIM
    nc = ts // c
    _load_conv_tile(q_ref, qpad, ts)
    _load_conv_tile(k_ref, kpad, ts)
    _load_conv_tile(v_ref, vpad, ts)

    @pl.when(pl.program_id(1) == 0)
    def _():
        state[...] = jnp.zeros(state.shape, F32)

    qc = _silu(_conv_from_pad(qpad, cwq_ref[...], ts))
    kc = _silu(_conv_from_pad(kpad, cwk_ref[...], ts))
    vc = _silu(_conv_from_pad(vpad, cwv_ref[...], ts))
    gates = gt_ref[0]
    neg_a = -jnp.exp(alog_ref[...])
    g_all = neg_a * _softplus(gates + dtb_ref[...])
    beta3_all = _sigmoid(gates).reshape(nc, c, GATE_LANES)
    gc3_all = _chunk_cumsum(g_all, c)
    gc_rows = _row_forms(gc3_all, [[(N_HEADS + h, 1.0)] for h in range(N_HEADS)])
    incl = _tril(c)
    strict = _tril(c, strict=True)
    eye = (lax.broadcasted_iota(jnp.int32, (c, c), 0)
           == lax.broadcasted_iota(jnp.int32, (c, c), 1)).astype(F32)

    q_u, k_u, kb_u, vb_u, kbe_u, qd_u, kd_u, dec_u, gl_u = ([] for _ in range(9))
    for h in range(N_HEADS):
        hs = slice(h * hd, (h + 1) * hd)
        qh = qc[:, hs]
        kh = kc[:, hs]
        qn = (qh * lax.rsqrt(jnp.sum(qh * qh, axis=-1, keepdims=True) + EPS)
              * (hd ** -0.5)).reshape(nc, c, hd)
        kn = (kh * lax.rsqrt(jnp.sum(kh * kh, axis=-1, keepdims=True) + EPS)).reshape(nc, c, hd)
        beta = _lane_pick3(beta3_all, h)
        gc = _lane_pick3(gc3_all, N_HEADS + h)
        e_gc = jnp.exp(gc)
        gc_last = gc[:, c - 1:c, :]
        kb = kn * beta
        q_u.append(qn.astype(BF16))
        k_u.append(kn.astype(BF16))
        kb_u.append(kb.astype(BF16))
        vb_u.append((vc[:, hs].reshape(nc, c, hd) * beta).astype(BF16))
        kbe_u.append((kb * e_gc).astype(BF16))
        qd_u.append(qn * e_gc)
        kd_u.append((kn * jnp.exp(gc_last - gc)).astype(BF16))
        dec_u.append(jnp.exp(jnp.where(incl, gc[:, :, :c] - gc_rows[h], NEG)))
        gl_u.append(jnp.exp(gc_last))
    cat = lambda parts: jnp.concatenate(parts, axis=0)
    q_u, k_u, kb_u, vb_u, kbe_u, qd_u, kd_u, dec_u = (
        cat(t) for t in (q_u, k_u, kb_u, vb_u, kbe_u, qd_u, kd_u, dec_u))
    gl_scr[...] = cat(gl_u)

    a_mat = jnp.where(strict, _bdot_nt(kb_u, k_u) * dec_u, 0.0)
    t_inv = eye - a_mat
    a_pow = a_mat
    for _ in range(int(math.log2(c)) - 1):
        a_pow = _bdot(a_pow, a_pow)
        t_inv = t_inv + _bdot(t_inv, a_pow)
    u = _bdot(t_inv, vb_u)
    w = _bdot(t_inv, kbe_u)
    qk = jnp.where(incl, _bdot_nt(q_u, k_u) * dec_u, 0.0)
    p_scr[...] = _bdot_tn(kd_u, w).astype(BF16)
    n_scr[...] = _bdot_tn(kd_u, u)
    qp_scr[...] = (qd_u - _bdot(qk, w)).astype(BF16)
    op_scr[...] = _bdot(qk, u)

    def step(ci, carry):
        rows = pl.ds(pl.multiple_of(ci * c, c), c)
        for h in range(N_HEADS):
            hs = slice(h * hd, (h + 1) * hd)
            idx = h * nc + ci
            st = state[h]
            stb = st.astype(BF16)
            o_scr[rows, hs] = jnp.dot(qp_scr[idx], stb, preferred_element_type=F32) + op_scr[idx]
            state[h] = (gl_scr[idx] * st - jnp.dot(p_scr[idx], stb, preferred_element_type=F32)
                        + n_scr[idx])
        return carry

    lax.fori_loop(0, nc, step, 0)
    y = _head_rms(o_scr[...], gn_ref[...]) * _silu(z_ref[0].astype(F32))
    o_ref[0] = y.astype(o_ref.dtype)


def gated_delta_mixer(proj, gates, conv_w, a_log_v, dt_bias_v, gn_c, ts=512):
    b, s, _ = proj.shape
    w = GROUP_WIDTH
    hd = HEAD_DIM
    units = N_HEADS * (ts // CHUNK)

    def col(off):
        return pl.BlockSpec((1, ts, w), lambda bi, i, off=off: (bi, i, off))

    def cw(off):
        return pl.BlockSpec((CONV_WIDTH, w), lambda bi, i, off=off: (0, off))

    vec = pl.BlockSpec((1, GATE_LANES), lambda bi, i: (0, 0))
    big = lambda: pltpu.VMEM((ts, w), F32)
    return pl.pallas_call(
        _gdn_kernel,
        grid=(b, s // ts),
        in_specs=[col(COL_CQ), col(COL_CK), col(COL_CV), col(COL_CZ),
                  pl.BlockSpec((1, ts, GATE_LANES), lambda bi, i: (bi, i, 0)),
                  cw(0), cw(1), cw(2), vec, vec,
                  pl.BlockSpec((1, w), lambda bi, i: (0, 0))],
        out_specs=pl.BlockSpec((1, ts, w), lambda bi, i: (bi, i, 0)),
        out_shape=jax.ShapeDtypeStruct((b, s, w), BF16),
        scratch_shapes=[pltpu.VMEM((ts + SUBLANES, w), F32) for _ in range(3)]
        + [pltpu.VMEM((units, hd, hd), BF16), pltpu.VMEM((units, hd, hd), F32),
           pltpu.VMEM((units, CHUNK, hd), BF16), pltpu.VMEM((units, CHUNK, hd), F32),
           pltpu.VMEM((units, 1, hd), F32), big(),
           pltpu.VMEM((N_HEADS, hd, hd), F32)],
        compiler_params=_cparams(("arbitrary", "arbitrary")),
        name="gated_delta_net",
    )(proj, proj, proj, proj, gates, conv_w, conv_w, conv_w, a_log_v, dt_bias_v, gn_c)


def _mlstm_kernel(q_ref, k_ref, v_ref, og_ref, gt_ref, ib_ref, fb_ref, gn_ref, o_ref,
                  kvn_scr, keep_scr, qcn_scr, h_scr, cn_state, m_state):
    ts = q_ref.shape[1]
    c = CHUNK
    hd = HEAD_DIM
    nc = ts // c

    @pl.when(pl.program_id(1) == 0)
    def _():
        cn_state[...] = jnp.zeros(cn_state.shape, F32)
        m_state[...] = jnp.full(m_state.shape, M_INIT, F32)

    gates = gt_ref[0]
    i_all = gates + ib_ref[...]
    f_pre = gates + fb_ref[...]
    lf_all = jnp.minimum(f_pre, 0.0) - jnp.log1p(jnp.exp(-jnp.abs(f_pre)))
    b3_all = _chunk_cumsum(lf_all, c)
    lane = lax.broadcasted_iota(jnp.int32, (nc, c, GATE_LANES), 2)
    ib3_all = jnp.where(lane < 3 * N_HEADS, i_all.reshape(nc, c, GATE_LANES), b3_all)
    ib_rows = _row_forms(ib3_all, [[(2 * N_HEADS + h, 1.0), (3 * N_HEADS + h, -1.0)]
                                   for h in range(N_HEADS)])
    incl = _tril(c)
    k_scale = 1.0 / math.sqrt(hd)
    ones_v = jnp.ones((nc, c, hd), BF16)

    per_head = []
    for h in range(N_HEADS):
        hs = slice(h * hd, (h + 1) * hd)
        q3 = q_ref[0, :, hs].reshape(nc, c, hd)
        k3 = k_ref[0, :, hs].astype(F32).reshape(nc, c, hd) * k_scale
        v3 = v_ref[0, :, hs].reshape(nc, c, hd)
        bb = _lane_pick3(b3_all, 3 * N_HEADS + h)
        ig = _lane_pick3(ib3_all, 2 * N_HEADS + h)
        log_d = jnp.where(incl, bb[:, :, :c] + ib_rows[h], NEG)
        max_ld = jnp.max(log_d, axis=-1, keepdims=True)
        b_last = bb[:, c - 1:c, :]
        log_w = b_last - bb + ig
        max_lw = jnp.max(log_w, axis=1, keepdims=True)
        m = m_state[h]
        m_prev, m_next = [], []
        for ci in range(nc):
            m_prev.append(m)
            m = jnp.maximum(b_last[ci] + m, max_lw[ci])
            m_next.append(m)
        m_state[h] = m
        m_prev = jnp.stack(m_prev, axis=0)
        m_next = jnp.stack(m_next, axis=0)
        inter = bb + m_prev
        m_t = jnp.maximum(inter, max_ld)
        d_mat = jnp.exp(log_d - m_t[:, :, :c])
        e_inter = jnp.exp(inter - m_t)
        sc = _bdot_nt(q3, k3) * d_mat
        scv = _bdot(sc, v3)
        rs = jnp.sum(sc, axis=-1, keepdims=True)
        kw = k3 * jnp.exp(log_w - m_next)
        kvn_scr[h * nc:(h + 1) * nc] = _bdot_tn(kw, jnp.concatenate([v3, ones_v], axis=2))
        keep = jnp.exp(b_last + m_prev - m_next)
        keep_scr[h * nc:(h + 1) * nc] = jnp.concatenate([keep, keep], axis=2)
        per_head.append((e_inter, scv, rs, jnp.exp(-m_t)))

    def step(ci, carry):
        rows = pl.ds(pl.multiple_of(ci * c, c), c)
        for h in range(N_HEADS):
            hs = slice(h * hd, (h + 1) * hd)
            idx = h * nc + ci
            cn = cn_state[h]
            qcn_scr[idx] = _dot(q_ref[0, rows, hs], cn)
            cn_state[h] = keep_scr[idx] * cn + kvn_scr[idx]
        return carry

    lax.fori_loop(0, nc, step, 0)
    for h in range(N_HEADS):
        hs = slice(h * hd, (h + 1) * hd)
        e_inter, scv, rs, e_mt = per_head[h]
        qcn = qcn_scr[h * nc:(h + 1) * nc]
        num = e_inter * qcn[:, :, :hd] + scv
        den = e_inter * qcn[:, :, hd:] + rs
        h_scr[:, hs] = (num / jnp.maximum(jnp.abs(den), e_mt)).reshape(ts, hd)
    y = _head_rms(h_scr[...], gn_ref[...]) * _sigmoid(og_ref[0].astype(F32))
    o_ref[0] = y.astype(o_ref.dtype)


def mlstm_mixer(proj, gates, i_bias_v, f_bias_v, gn_d, ts=512):
    b, s, _ = proj.shape
    w = GROUP_WIDTH
    hd = HEAD_DIM
    units = N_HEADS * (ts // CHUNK)

    def col(off):
        return pl.BlockSpec((1, ts, w), lambda bi, i, off=off: (bi, i, off))

    vec = pl.BlockSpec((1, GATE_LANES), lambda bi, i: (0, 0))
    big = lambda: pltpu.VMEM((ts, w), F32)
    return pl.pallas_call(
        _mlstm_kernel,
        grid=(b, s // ts),
        in_specs=[col(COL_DQ), col(COL_DK), col(COL_DV), col(COL_DO),
                  pl.BlockSpec((1, ts, GATE_LANES), lambda bi, i: (bi, i, 0)),
                  vec, vec, pl.BlockSpec((1, w), lambda bi, i: (0, 0))],
        out_specs=pl.BlockSpec((1, ts, w), lambda bi, i: (bi, i, 0)),
        out_shape=jax.ShapeDtypeStruct((b, s, w), BF16),
        scratch_shapes=[pltpu.VMEM((units, hd, 2 * hd), F32), pltpu.VMEM((units, 1, 2 * hd), F32),
                        pltpu.VMEM((units, CHUNK, 2 * hd), F32), big(),
                        pltpu.VMEM((N_HEADS, hd, 2 * hd), F32),
                        pltpu.VMEM((N_HEADS, 1, hd), F32)],
        compiler_params=_cparams(("arbitrary", "arbitrary")),
        name="mlstm",
    )(proj, proj, proj, proj, gates, i_bias_v, f_bias_v, gn_d)


SLAB = 16
SLAB_LANES = 128
MOE_TM = 256
META_ROWS = 8


def _route(h, wr, rb_col):
    tm = h.shape[0]
    logits = lax.dot_general(wr, h, (((1,), (1,)), ((), ())),
                             precision=HIGHEST, preferred_element_type=F32)
    scores = _sigmoid(logits)
    biased = scores + rb_col
    sr = [scores[e:e + 1, :] for e in range(N_EXPERTS)]
    br = [biased[e:e + 1, :] for e in range(N_EXPERTS)]
    n_groups = N_EXPERTS // EXPERTS_PER_GROUP
    group_scores = []
    for g in range(n_groups):
        a, b, cc, d = br[4 * g:4 * g + 4]
        p, q = jnp.maximum(a, b), jnp.minimum(a, b)
        r, s = jnp.maximum(cc, d), jnp.minimum(cc, d)
        group_scores.append(jnp.maximum(p, r) + jnp.maximum(jnp.minimum(p, r), jnp.maximum(q, s)))
    sel = jnp.zeros((1, tm), jnp.int32)
    best = group_scores[0]
    for g in range(1, n_groups):
        better = group_scores[g] > best
        sel = jnp.where(better, g, sel)
        best = jnp.where(better, group_scores[g], best)
    masked = [jnp.where(sel == (e // EXPERTS_PER_GROUP), br[e], -jnp.inf) for e in range(N_EXPERTS)]

    def top1(vals):
        idx = jnp.zeros((1, tm), jnp.int32)
        bv = vals[0]
        for e in range(1, N_EXPERTS):
            better = vals[e] > bv
            idx = jnp.where(better, e, idx)
            bv = jnp.where(better, vals[e], bv)
        return idx

    i1 = top1(masked)
    i2 = top1([jnp.where(i1 == e, -jnp.inf, masked[e]) for e in range(N_EXPERTS)])
    s1 = sum(jnp.where(i1 == e, sr[e], 0.0) for e in range(N_EXPERTS))
    s2 = sum(jnp.where(i2 == e, sr[e], 0.0) for e in range(N_EXPERTS))
    tot = s1 + s2
    return i1, i2, s1 / tot, s2 / tot


def _router_kernel(x_ref, sc_ref, sh_ref, nw_ref, wr_ref, rb_ref, hs_ref, meta_ref, wts_ref, cnt_ref,
                   run):
    @pl.when((pl.program_id(0) == 0) & (pl.program_id(1) == 0))
    def _():
        run[...] = jnp.zeros(run.shape, F32)

    h = _modulated_norm(x_ref[0], nw_ref[...], sc_ref[0], sh_ref[0])
    tm = h.shape[0]
    for s in range(SLAB):
        hs_ref[pl.ds(s, tm, stride=SLAB), :] = h[:, s * SLAB_LANES:(s + 1) * SLAB_LANES]
    i1, i2, w1, w2 = _route(h, wr_ref[...], rb_ref[...][:, 0:1])
    oh1 = jnp.concatenate([(i1 == e).astype(F32) for e in range(N_EXPERTS)], axis=0)
    oh2 = jnp.concatenate([(i2 == e).astype(F32) for e in range(N_EXPERTS)], axis=0)
    before = (lax.broadcasted_iota(jnp.int32, (tm, tm), 0)
              < lax.broadcasted_iota(jnp.int32, (tm, tm), 1)).astype(BF16)
    excl1 = jnp.dot(oh1.astype(BF16), before, preferred_element_type=F32)
    excl2 = jnp.dot(oh2.astype(BF16), before, preferred_element_type=F32)
    tot1 = jnp.sum(oh1, axis=1, keepdims=True)
    tot2 = jnp.sum(oh2, axis=1, keepdims=True)
    base = run[...][:, 0:1]
    rank1 = jnp.sum(oh1 * (base + excl1), axis=0, keepdims=True)
    rank2 = jnp.sum(oh2 * (base + tot1 + excl2), axis=0, keepdims=True)
    run[...] = run[...] + (tot1 + tot2)
    cnt_ref[...] = run[...]
    meta_ref[...] = jnp.concatenate(
        [i1, i2, rank1.astype(jnp.int32), rank2.astype(jnp.int32),
         jnp.zeros((META_ROWS - 4, tm), jnp.int32)], axis=0)
    wts_ref[...] = jnp.concatenate([w1, w2, jnp.zeros((META_ROWS - 2, tm), F32)], axis=0)


def router(x, scale, shift, norm_w, w_router_t, router_bias_col, tm=512):
    b, s, d = x.shape
    nt = s // tm
    t = b * s
    return pl.pallas_call(
        _router_kernel,
        grid=(b, nt),
        in_specs=[pl.BlockSpec((1, tm, d), lambda bi, i: (bi, i, 0)),
                  pl.BlockSpec((1, 1, d), lambda bi, i: (bi, 0, 0)),
                  pl.BlockSpec((1, 1, d), lambda bi, i: (bi, 0, 0)),
                  pl.BlockSpec((1, d), lambda bi, i: (0, 0)),
                  pl.BlockSpec((N_EXPERTS, d), lambda bi, i: (0, 0)),
                  pl.BlockSpec((N_EXPERTS, GATE_LANES), lambda bi, i: (0, 0))],
        out_specs=[pl.BlockSpec((tm * SLAB, SLAB_LANES), lambda bi, i: (bi * nt + i, 0)),
                   pl.BlockSpec((META_ROWS, tm), lambda bi, i: (0, bi * nt + i)),
                   pl.BlockSpec((META_ROWS, tm), lambda bi, i: (0, bi * nt + i)),
                   pl.BlockSpec((N_EXPERTS, GATE_LANES), lambda bi, i: (0, 0))],
        out_shape=[jax.ShapeDtypeStruct((t * SLAB, SLAB_LANES), F32),
                   jax.ShapeDtypeStruct((META_ROWS, t), jnp.int32),
                   jax.ShapeDtypeStruct((META_ROWS, t), F32),
                   jax.ShapeDtypeStruct((N_EXPERTS, GATE_LANES), F32)],
        scratch_shapes=[pltpu.VMEM((N_EXPERTS, GATE_LANES), F32)],
        compiler_params=_cparams(("arbitrary", "arbitrary")),
        name="router",
    )(x, scale, shift, norm_w, w_router_t, router_bias_col)


def routing_tables(meta, counts, tm=MOE_TM):
    t = meta.shape[1]
    cnt = counts[:, 0].astype(jnp.int32)
    padded = (cnt + tm - 1) // tm * tm
    ends = jnp.cumsum(padded)
    off = ends - padded
    eid = meta[0:2]
    pair_off = sum(jnp.where(eid == e, off[e], 0) for e in range(N_EXPERTS))
    pos = pair_off + meta[2:4]
    n_tiles = 2 * t // tm + N_EXPERTS
    tile_start = jnp.arange(n_tiles, dtype=jnp.int32) * tm
    tile_expert = jnp.minimum(jnp.sum((ends[None, :] <= tile_start[:, None]).astype(jnp.int32), axis=1),
                              N_EXPERTS - 1).astype(jnp.int32)
    n_valid = (ends[-1] // tm).astype(jnp.int32).reshape(1)
    return pos.reshape(-1).astype(jnp.int32), tile_expert, n_valid


def _slab_rows(i):
    return pl.ds(pl.multiple_of(i * SLAB, SLAB), SLAB)


def _dispatch_kernel(pos_ref, hs_ref, xz_ref, xs_ref, pair_ref, sem):
    del xz_ref
    tmd = hs_ref.shape[0] // SLAB
    t = pos_ref.shape[0] // 2
    base = pl.program_id(0) * tmd

    @pl.when(pl.program_id(0) == 0)
    def _():
        def fill(r, carry):
            pair_ref[r] = -1
            return carry

        lax.fori_loop(0, pair_ref.shape[0], fill, 0, unroll=8)

    def issue(r, carry):
        for k in range(2):
            p = pos_ref[k * t + base + r]
            pair_ref[p] = k * t + base + r
            pltpu.make_async_copy(hs_ref.at[_slab_rows(r), :], xs_ref.at[_slab_rows(p), :], sem).start()
        return carry

    lax.fori_loop(0, tmd, issue, 0, unroll=8)
    for k in range(2):
        pltpu.make_async_copy(hs_ref, xs_ref.at[pl.ds(0, tmd * SLAB), :], sem).wait()


def dispatch(pos, hs, xs_init, tmd=512):
    t = hs.shape[0] // SLAB
    return pl.pallas_call(
        _dispatch_kernel,
        grid_spec=pltpu.PrefetchScalarGridSpec(
            num_scalar_prefetch=1,
            grid=(t // tmd,),
            in_specs=[pl.BlockSpec((tmd * SLAB, SLAB_LANES), lambda i, pos: (i, 0)),
                      pl.BlockSpec(memory_space=pl.ANY)],
            out_specs=[pl.BlockSpec(memory_space=pl.ANY), pl.BlockSpec(memory_space=pltpu.SMEM)],
            scratch_shapes=[pltpu.SemaphoreType.DMA]),
        out_shape=[jax.ShapeDtypeStruct(xs_init.shape, F32),
                   jax.ShapeDtypeStruct((xs_init.shape[0] // SLAB,), jnp.int32)],
        input_output_aliases={2: 0},
        compiler_params=_cparams(("arbitrary",)),
        name="moe_dispatch",
    )(pos, hs, xs_init)


def _experts_kernel(te_ref, nv_ref, pair_ref, xs_ref, wg_ref, wu_ref, wd_ref, yi_ref, yt_ref,
                    wgb, wub, wdb, yblk, sem):
    del yi_ref
    j = pl.program_id(0)
    nv = nv_ref[0]
    tm = xs_ref.shape[0] // SLAB

    def wait_rows():
        pltpu.make_async_copy(yblk, yt_ref.at[pl.ds(0, tm * SLAB), :], sem).wait()

    @pl.when(j < nv)
    def _():
        prev = te_ref[jnp.maximum(j - 1, 0)]

        @pl.when((j == 0) | (prev != te_ref[j]))
        def _():
            wgb[...] = wg_ref[0, 0].astype(BF16)
            wub[...] = wu_ref[0, 0].astype(BF16)
            wdb[...] = wd_ref[0, 0].astype(BF16)

        x = jnp.concatenate([xs_ref[pl.ds(s, tm, stride=SLAB), :] for s in range(SLAB)],
                            axis=1).astype(BF16)
        hg = jnp.dot(x, wgb[...], preferred_element_type=F32)
        hu = jnp.dot(x, wub[...], preferred_element_type=F32)
        y = _dot(_silu(hg) * hu, wdb[...])

        @pl.when(j > 0)
        def _():
            wait_rows()

        for s in range(SLAB):
            yblk[pl.ds(s, tm, stride=SLAB), :] = y[:, s * SLAB_LANES:(s + 1) * SLAB_LANES]

        spare = yt_ref.shape[0] // SLAB - N_EXPERTS * tm + te_ref[j] * tm

        def issue(r, carry):
            pair = pair_ref[j * tm + r]
            d = jnp.where(pair < 0, spare + r, pair)
            pltpu.make_async_copy(yblk.at[_slab_rows(r), :], yt_ref.at[_slab_rows(d), :], sem).start()
            return carry

        lax.fori_loop(0, tm, issue, 0, unroll=8)

        @pl.when(j == nv - 1)
        def _():
            wait_rows()


def grouped_experts(tile_expert, n_valid, row_pair, xs, w_gate, w_up, w_down, layer, yt_init,
                    tm=MOE_TM):
    _, ne, d, dff = w_gate.shape
    n_tiles = tile_expert.shape[0]

    def row_tile(j, te, nv, pair):
        return (jnp.minimum(j, nv[0] - 1), 0)

    def expert(j, te, nv, pair):
        return (layer, te[jnp.minimum(j, nv[0] - 1)], 0, 0)

    return pl.pallas_call(
        _experts_kernel,
        grid_spec=pltpu.PrefetchScalarGridSpec(
            num_scalar_prefetch=3,
            grid=(n_tiles,),
            in_specs=[pl.BlockSpec((tm * SLAB, SLAB_LANES), row_tile),
                      pl.BlockSpec((1, 1, d, dff), expert),
                      pl.BlockSpec((1, 1, d, dff), expert),
                      pl.BlockSpec((1, 1, dff, d), expert),
                      pl.BlockSpec(memory_space=pl.ANY)],
            out_specs=pl.BlockSpec(memory_space=pl.ANY),
            scratch_shapes=[pltpu.VMEM((d, dff), BF16), pltpu.VMEM((d, dff), BF16),
                            pltpu.VMEM((dff, d), BF16), pltpu.VMEM((tm * SLAB, SLAB_LANES), F32),
                            pltpu.SemaphoreType.DMA]),
        out_shape=jax.ShapeDtypeStruct(yt_init.shape, F32),
        input_output_aliases={7: 0},
        compiler_params=_cparams(("arbitrary",)),
        name="moe_experts",
    )(tile_expert, n_valid, row_pair, xs, w_gate, w_up, w_down, yt_init)


def _combine_kernel(x_ref, y1_ref, y2_ref, w_ref, gate_ref, nw_ref, o_ref, *, final_norm):
    tm = x_ref.shape[1]
    w = w_ref[...]
    w1 = w[:, 0:1]
    w2 = w[:, 1:2]
    parts = [w1 * y1_ref[pl.ds(s, tm, stride=SLAB), :] + w2 * y2_ref[pl.ds(s, tm, stride=SLAB), :]
             for s in range(SLAB)]
    out = x_ref[0] + gate_ref[0] * jnp.concatenate(parts, axis=1)
    if final_norm:
        ms = jnp.mean(out * out, axis=-1, keepdims=True)
        out = out * lax.rsqrt(ms + EPS) * nw_ref[...]
    o_ref[0] = out


def combine(x, yt, w_cols, gate, norm_w, final_norm, tm=256):
    b, s, d = x.shape
    nt = s // tm
    t = b * s
    return pl.pallas_call(
        functools.partial(_combine_kernel, final_norm=final_norm),
        grid=(b, nt),
        in_specs=[pl.BlockSpec((1, tm, d), lambda bi, i: (bi, i, 0)),
                  pl.BlockSpec((tm * SLAB, SLAB_LANES), lambda bi, i: (bi * nt + i, 0)),
                  pl.BlockSpec((tm * SLAB, SLAB_LANES), lambda bi, i: (t // tm + bi * nt + i, 0)),
                  pl.BlockSpec((tm, META_ROWS), lambda bi, i: (bi * nt + i, 0)),
                  pl.BlockSpec((1, 1, d), lambda bi, i: (bi, 0, 0)),
                  pl.BlockSpec((1, d), lambda bi, i: (0, 0))],
        out_specs=pl.BlockSpec((1, tm, d), lambda bi, i: (bi, i, 0)),
        out_shape=jax.ShapeDtypeStruct((b, s, d), F32),
        compiler_params=_cparams(("arbitrary", "arbitrary")),
        name="moe_combine",
    )(x, yt, yt, w_cols, gate, norm_w)


def _split_w_in(w_in):
    gw = GROUP_WIDTH
    a_end = 3 * gw
    b_end = a_end + 2 * gw
    c_main_end = b_end + 4 * gw
    c_end = c_main_end + 2 * N_HEADS
    d_main_end = c_end + 4 * gw
    main = jnp.concatenate([w_in[:, :c_main_end], w_in[:, c_end:d_main_end]], axis=1)
    small = jnp.concatenate([w_in[:, c_main_end:c_end], w_in[:, d_main_end:]], axis=1)
    small = jnp.pad(small, ((0, 0), (0, GATE_LANES - small.shape[1])))
    return main.astype(BF16), small.astype(BF16)


def _lanes(vec, offset):
    return jnp.zeros((1, GATE_LANES), F32).at[0, offset:offset + N_HEADS].set(vec)


def _rope_tables(s):
    half = HEAD_DIM // 2
    inv_freq = ROPE_THETA ** (-jnp.arange(half, dtype=F32) / half)
    ang = jnp.arange(s, dtype=F32)[:, None] * inv_freq[None, :]
    cos, sin = jnp.cos(ang), jnp.sin(ang)
    return jnp.concatenate([cos, cos], axis=1), jnp.concatenate([-sin, sin], axis=1)


def hybrid_mixer_layer(x, sc1, sh1, g1, norm_w, w_in, w_out, gn_a, conv_b_w, conv_b_b, w_rg, b_rg,
                       w_ig, b_ig, lru_lambda, gn_b, conv_c_w, gdn_a_log, gdn_dt_bias, gdn_norm,
                       mlstm_i_bias, mlstm_f_bias, gn_d, rope):
    w_main, w_small = _split_w_in(w_in)
    proj, gates = in_projection(x, sc1, sh1, norm_w.reshape(1, -1), w_main, w_small)
    row = lambda v: v.reshape(1, -1)
    y_a = dilated_attention_mixer(proj, rope[0], rope[1], row(gn_a))
    y_b = rg_lru_mixer(proj, conv_b_w, row(conv_b_b), w_rg.astype(BF16), row(b_rg),
                       w_ig.astype(BF16), row(b_ig), row(lru_lambda), row(gn_b))
    y_c = gated_delta_mixer(proj, gates, conv_c_w, _lanes(gdn_a_log, N_HEADS),
                            _lanes(gdn_dt_bias, N_HEADS), row(jnp.tile(gdn_norm, N_HEADS)))
    y_d = mlstm_mixer(proj, gates, _lanes(mlstm_i_bias, 2 * N_HEADS),
                      _lanes(mlstm_f_bias, 3 * N_HEADS), row(gn_d))
    return out_projection(x, (y_a, y_b, y_c, y_d), w_out.astype(BF16), g1)


def moe_layer(x, sc2, sh2, g2, norm_w, w_router, router_bias, w_gate, w_up, w_down, layer, xs_buf,
              yt_buf, final_norm_w, apply_final_norm):
    rb = jnp.broadcast_to(router_bias.reshape(-1, 1), (N_EXPERTS, GATE_LANES))
    hs, meta, wts, counts = router(x, sc2, sh2, norm_w.reshape(1, -1), w_router.T, rb)
    pos, tile_expert, n_valid = routing_tables(meta, counts)
    xs, row_pair = dispatch(pos, hs, xs_buf)
    yt = grouped_experts(tile_expert, n_valid, row_pair, xs, w_gate, w_up, w_down, layer, yt_buf)
    return combine(x, yt, wts.T, g2, final_norm_w, apply_final_norm), xs, yt


def moe_buffers(n_tokens):
    rows = 2 * n_tokens + N_EXPERTS * MOE_TM
    return jnp.zeros((rows * SLAB, SLAB_LANES), F32), jnp.zeros((rows * SLAB, SLAB_LANES), F32)


def kernel(x, c, norm_mix, norm_ffn, norm_final, w_ada, b_ada, w_in, w_out, gn_a, conv_b_w, conv_b_b, w_rg, b_rg, w_ig, b_ig, lru_lambda, gn_b, conv_c_w, gdn_a_log, gdn_dt_bias, gdn_norm, mlstm_i_bias, mlstm_f_bias, gn_d, w_router, router_bias, moe_w_gate, moe_w_up, moe_w_down):
    depth = w_ada.shape[0]
    b, s, d = x.shape
    mod = ada_modulation(c, w_ada, b_ada)
    rope = _rope_tables(s)
    xs_buf, yt_buf = moe_buffers(b * s)
    for l in range(depth):
        sh1, sc1, g1, sh2, sc2, g2 = [mod[l, :, None, i * d:(i + 1) * d] for i in range(6)]
        x = hybrid_mixer_layer(x, sc1, sh1, g1, norm_mix[l], w_in[l], w_out[l], gn_a[l],
                               conv_b_w[l], conv_b_b[l], w_rg[l], b_rg[l], w_ig[l], b_ig[l],
                               lru_lambda[l], gn_b[l], conv_c_w[l], gdn_a_log[l], gdn_dt_bias[l],
                               gdn_norm[l], mlstm_i_bias[l], mlstm_f_bias[l], gn_d[l], rope)
        x, xs_buf, yt_buf = moe_layer(x, sc2, sh2, g2, norm_ffn[l], w_router, router_bias,
                                      moe_w_gate, moe_w_up, moe_w_down, l, xs_buf, yt_buf,
                                      norm_final.reshape(1, -1), l == depth - 1)
    return x
```

```python
import functools
import math

import jax
import jax.numpy as jnp
from jax import lax
from jax.experimental import pallas as pl
from jax.experimental.pallas import tpu as pltpu

F32 = jnp.float32
BF16 = jnp.bfloat16
HIGHEST = lax.Precision.HIGHEST

HEAD_DIM = 128
N_HEADS = 4
GROUP_WIDTH = N_HEADS * HEAD_DIM
DILATED_PAIRS = ((128, 1), (512, 4), (2048, 16))
ATTN_BLOCK = 128
ROPE_THETA = 10000.0
CONV_WIDTH = 4
LRU_C = 8.0
CHUNK = 64
N_EXPERTS = 16
EXPERTS_PER_GROUP = 4
EPS = 1e-6
M_INIT = -1e30
NEG = -1e30
SUBLANES = 8
VMEM_LIMIT = 56 * 1024 * 1024

COL_AQ, COL_AK, COL_AV = 0, 4, 8
COL_BX, COL_BG = 3, 4
COL_CQ, COL_CK, COL_CV, COL_CZ = 5, 6, 7, 8
COL_DQ, COL_DK, COL_DV, COL_DO = 9, 10, 11, 12
MAIN_COLS = 13 * GROUP_WIDTH
GATE_LANES = 128


def _cparams(sem):
    return pltpu.CompilerParams(dimension_semantics=sem, vmem_limit_bytes=VMEM_LIMIT)


def _dot(a, b):
    return jnp.dot(a.astype(BF16), b.astype(BF16), preferred_element_type=F32)


def _dot_nt(a, b):
    return lax.dot_general(a.astype(BF16), b.astype(BF16), (((1,), (1,)), ((), ())),
                           preferred_element_type=F32)


def _softplus(z):
    return jnp.maximum(z, 0.0) + jnp.log1p(jnp.exp(-jnp.abs(z)))


def _sigmoid(z):
    return 1.0 / (1.0 + jnp.exp(-z))


def _silu(z):
    return z * _sigmoid(z)


def _head_rms(x, gain):
    outs = []
    for g in range(x.shape[1] // HEAD_DIM):
        xg = x[:, g * HEAD_DIM:(g + 1) * HEAD_DIM]
        ms = jnp.mean(xg * xg, axis=-1, keepdims=True)
        outs.append(xg * lax.rsqrt(ms + EPS))
    y = outs[0] if len(outs) == 1 else jnp.concatenate(outs, axis=1)
    return y * gain


def _tril(c, strict=False):
    row = lax.broadcasted_iota(jnp.int32, (c, c), 0)
    col = lax.broadcasted_iota(jnp.int32, (c, c), 1)
    return (col < row) if strict else (col <= row)


def _ada_kernel(c_ref, w_ref, b_ref, o_ref):
    c = c_ref[...]
    o_ref[0] = _dot(_silu(c), w_ref[0]) + b_ref[0]


def ada_modulation(c, w_ada, b_ada):
    depth, d, n = w_ada.shape
    b = c.shape[0]
    rows = -(-b // SUBLANES) * SUBLANES
    c_pad = jnp.zeros((rows, d), F32).at[:b].set(c)
    tn = 1024
    out = pl.pallas_call(
        _ada_kernel,
        grid=(depth, n // tn),
        in_specs=[pl.BlockSpec((rows, d), lambda l, j: (0, 0)),
                  pl.BlockSpec((1, d, tn), lambda l, j: (l, 0, j)),
                  pl.BlockSpec((1, 1, tn), lambda l, j: (l, 0, j))],
        out_specs=pl.BlockSpec((1, rows, tn), lambda l, j: (l, 0, j)),
        out_shape=jax.ShapeDtypeStruct((depth, rows, n), F32),
        compiler_params=_cparams(("arbitrary", "arbitrary")),
        name="ada_modulation",
    )(c_pad, w_ada, b_ada.reshape(depth, 1, n))
    return out[:, :b]


def _modulated_norm(x, nw, sc, sh):
    ms = jnp.mean(x * x, axis=-1, keepdims=True)
    return (x * lax.rsqrt(ms + EPS) * nw) * (1.0 + sc) + sh


def _in_proj_kernel(x_ref, sc_ref, sh_ref, nw_ref, w_ref, wg_ref, proj_ref, gates_ref, h_scr):
    @pl.when(pl.program_id(2) == 0)
    def _():
        h = _modulated_norm(x_ref[0], nw_ref[...], sc_ref[0], sh_ref[0]).astype(BF16)
        h_scr[...] = h
        gates_ref[0] = jnp.dot(h, wg_ref[...], preferred_element_type=F32)

    proj_ref[0] = jnp.dot(h_scr[...], w_ref[...], preferred_element_type=F32).astype(BF16)


def in_projection(x, scale, shift, norm_w, w_main, w_gate, tm=512, tn=1664):
    b, s, d = x.shape
    n = w_main.shape[1]
    return pl.pallas_call(
        _in_proj_kernel,
        grid=(b, s // tm, n // tn),
        in_specs=[pl.BlockSpec((1, tm, d), lambda bi, i, j: (bi, i, 0)),
                  pl.BlockSpec((1, 1, d), lambda bi, i, j: (bi, 0, 0)),
                  pl.BlockSpec((1, 1, d), lambda bi, i, j: (bi, 0, 0)),
                  pl.BlockSpec((1, d), lambda bi, i, j: (0, 0)),
                  pl.BlockSpec((d, tn), lambda bi, i, j: (0, j)),
                  pl.BlockSpec((d, GATE_LANES), lambda bi, i, j: (0, 0))],
        out_specs=[pl.BlockSpec((1, tm, tn), lambda bi, i, j: (bi, i, j)),
                   pl.BlockSpec((1, tm, GATE_LANES), lambda bi, i, j: (bi, i, 0))],
        out_shape=[jax.ShapeDtypeStruct((b, s, n), BF16),
                   jax.ShapeDtypeStruct((b, s, GATE_LANES), F32)],
        scratch_shapes=[pltpu.VMEM((tm, d), BF16)],
        compiler_params=_cparams(("arbitrary", "arbitrary", "arbitrary")),
        name="in_projection",
    )(x, scale, shift, norm_w, w_main, w_gate)


def _out_proj_kernel(x_ref, ya_ref, yb_ref, yc_ref, yd_ref, w_ref, g_ref, o_ref):
    acc = None
    for k, y_ref in enumerate((ya_ref, yb_ref, yc_ref, yd_ref)):
        part = jnp.dot(y_ref[0], w_ref[k * GROUP_WIDTH:(k + 1) * GROUP_WIDTH, :],
                       preferred_element_type=F32)
        acc = part if acc is None else acc + part
    o_ref[0] = x_ref[0] + g_ref[0] * acc


def out_projection(x, ys, w_out, gate, tm=512):
    b, s, d = x.shape
    yspec = pl.BlockSpec((1, tm, GROUP_WIDTH), lambda bi, i: (bi, i, 0))
    return pl.pallas_call(
        _out_proj_kernel,
        grid=(b, s // tm),
        in_specs=[pl.BlockSpec((1, tm, d), lambda bi, i: (bi, i, 0)),
                  yspec, yspec, yspec, yspec,
                  pl.BlockSpec(w_out.shape, lambda bi, i: (0, 0)),
                  pl.BlockSpec((1, 1, d), lambda bi, i: (bi, 0, 0))],
        out_specs=pl.BlockSpec((1, tm, d), lambda bi, i: (bi, i, 0)),
        out_shape=jax.ShapeDtypeStruct((b, s, d), F32),
        compiler_params=_cparams(("arbitrary", "arbitrary")),
        name="out_projection",
    )(x, *ys, w_out, gate)


def _attn_kernel(q_ref, k_ref, v_ref, cos_ref, sin_ref, gn_ref, o_ref, qs, ks, vs, acc, ms, ls,
                 q4, k4, v4):
    s = q_ref.shape[1]
    blk = ATTN_BLOCK
    cos = cos_ref[...]
    sin = sin_ref[...]

    def rot(t):
        return t * cos + pltpu.roll(t, HEAD_DIM // 2, 1) * sin

    qs[...] = rot(q_ref[0].astype(F32)) * (1.0 / math.sqrt(HEAD_DIM))
    ks[...] = rot(k_ref[0].astype(F32))
    vs[...] = v_ref[0].astype(F32)

    row = lax.broadcasted_iota(jnp.int32, (blk, 2 * blk), 0)
    col = lax.broadcasted_iota(jnp.int32, (blk, 2 * blk), 1)
    own_ok = (col >= blk) & (col - blk <= row)

    def run_branch(refs, stride, span, locate, first_branch):
        q_s, k_s, v_s, acc_s, m_s, l_s = refs

        def rows(start):
            if stride == 1:
                return pl.ds(pl.multiple_of(start, blk), blk)
            return pl.ds(start, blk, stride=stride)

        def body(idx, carry):
            start, has_prev = locate(idx)
            pstart = jnp.where(has_prev, start - span, start)
            qb = q_s[rows(start), :]
            kc = jnp.concatenate([k_s[rows(pstart), :], k_s[rows(start), :]], axis=0)
            vc = jnp.concatenate([v_s[rows(pstart), :], v_s[rows(start), :]], axis=0)
            sc = _dot_nt(qb, kc)
            first = jnp.where(has_prev, 0, 4 * blk)
            valid = own_ok | ((col < blk) & (col >= row + first))
            sc = jnp.where(valid, sc, NEG)
            mb = jnp.max(sc, axis=-1, keepdims=True)
            p = jnp.exp(sc - mb)
            den = jnp.sum(p, axis=-1, keepdims=True)
            num = _dot(p, vc)
            mb = jnp.broadcast_to(mb, (blk, HEAD_DIM))
            den = jnp.broadcast_to(den, (blk, HEAD_DIM))
            if first_branch:
                acc_s[rows(start), :] = num
                m_s[rows(start), :] = mb
                l_s[rows(start), :] = den
            else:
                m_old = m_s[rows(start), :]
                m_new = jnp.maximum(m_old, mb)
                a_old = jnp.exp(m_old - m_new)
                a_blk = jnp.exp(mb - m_new)
                acc_s[rows(start), :] = acc_s[rows(start), :] * a_old + num * a_blk
                l_s[rows(start), :] = l_s[rows(start), :] * a_old + den * a_blk
                m_s[rows(start), :] = m_new
            return carry

        lax.fori_loop(0, s // blk, body, 0, unroll=8)

    def natural(dil):
        nb = s // (blk * dil)

        def locate(idx):
            n = idx % nb
            return idx // nb + n * (blk * dil), n > 0

        return locate

    (_, d0), (_, d1), (_, d2) = DILATED_PAIRS
    run_branch((qs, ks, vs, acc, ms, ls), d0, blk * d0, natural(d0), True)
    run_branch((qs, ks, vs, acc, ms, ls), d1, blk * d1, natural(d1), False)

    fold = 4
    seg = s // fold
    sub = d2 // fold

    def deinterleave(src, dst):
        for r1 in range(fold):
            dst[r1 * seg:(r1 + 1) * seg, :] = src[pl.ds(r1, seg, stride=fold), :]

    deinterleave(qs, q4)
    deinterleave(ks, k4)
    deinterleave(vs, v4)
    deinterleave(acc, qs)
    deinterleave(ms, ks)
    deinterleave(ls, vs)
    nb2 = s // (blk * d2)

    def folded(idx):
        r = idx // nb2
        n = idx % nb2
        return (r % fold) * seg + r // fold + n * (blk * sub), n > 0

    run_branch((q4, k4, v4, qs, ks, vs), sub, blk * sub, folded, False)
    y4 = _head_rms(qs[...] / vs[...], gn_ref[...])
    for r1 in range(fold):
        acc[pl.ds(r1, seg, stride=fold), :] = y4[r1 * seg:(r1 + 1) * seg, :]
    o_ref[0] = acc[...].astype(o_ref.dtype)


def dilated_attention_mixer(proj, cos2, sin2, gn_a):
    b, s, _ = proj.shape
    hd = HEAD_DIM

    def col(off):
        return pl.BlockSpec((1, s, hd), lambda bi, h, off=off: (bi, 0, off + h))

    return pl.pallas_call(
        _attn_kernel,
        grid=(b, N_HEADS),
        in_specs=[col(COL_AQ), col(COL_AK), col(COL_AV),
                  pl.BlockSpec((s, hd), lambda bi, h: (0, 0)),
                  pl.BlockSpec((s, hd), lambda bi, h: (0, 0)),
                  pl.BlockSpec((1, hd), lambda bi, h: (0, h))],
        out_specs=pl.BlockSpec((1, s, hd), lambda bi, h: (bi, 0, h)),
        out_shape=jax.ShapeDtypeStruct((b, s, GROUP_WIDTH), BF16),
        scratch_shapes=[pltpu.VMEM((s, hd), F32) for _ in range(9)],
        compiler_params=_cparams(("arbitrary", "arbitrary")),
        name="dilated_attention",
    )(proj, proj, proj, cos2, sin2, gn_a)


def _load_conv_tile(x_ref, xpad, ts):
    @pl.when(pl.program_id(1) == 0)
    def _():
        xpad[0:SUBLANES, :] = jnp.zeros((SUBLANES, xpad.shape[1]), F32)

    @pl.when(pl.program_id(1) > 0)
    def _():
        xpad[0:SUBLANES, :] = xpad[ts:ts + SUBLANES, :]

    xpad[SUBLANES:SUBLANES + ts, :] = x_ref[0].astype(F32)


def _conv_from_pad(xpad, w, ts):
    y = None
    for j in range(CONV_WIDTH):
        off = SUBLANES - (CONV_WIDTH - 1) + j
        term = xpad[off:off + ts, :] * w[j:j + 1, :]
        y = term if y is None else y + term
    return y


def _lru_kernel(x_ref, g_ref, cw_ref, cb_ref, wr_ref, br_ref, wi_ref, bi_ref, lam_ref, gn_ref,
                o_ref, xpad, a_scr, u_scr, carry):
    ts = x_ref.shape[1]
    w = x_ref.shape[2]
    _load_conv_tile(x_ref, xpad, ts)

    @pl.when(pl.program_id(1) == 0)
    def _():
        carry[...] = jnp.zeros(carry.shape, F32)

    xc = _conv_from_pad(xpad, cw_ref[...], ts) + cb_ref[...]
    r_parts, i_parts = [], []
    for g in range(N_HEADS):
        xg = xc[:, g * HEAD_DIM:(g + 1) * HEAD_DIM]
        r_parts.append(_dot(xg, wr_ref[g]))
        i_parts.append(_dot(xg, wi_ref[g]))
    r = _sigmoid(jnp.concatenate(r_parts, axis=1) + br_ref[...])
    ig = _sigmoid(jnp.concatenate(i_parts, axis=1) + bi_ref[...])
    log_a = (-LRU_C) * r * _softplus(-lam_ref[...])
    a_scr[...] = jnp.exp(log_a)
    u_scr[...] = jnp.sqrt(1.0 - jnp.exp(2.0 * log_a)) * (ig * xc)

    sub = lax.broadcasted_iota(jnp.int32, (SUBLANES, w), 0)

    def body(j, h_prev):
        rows = pl.ds(pl.multiple_of(j * SUBLANES, SUBLANES), SUBLANES)
        a = a_scr[rows, :]
        u = u_scr[rows, :]
        for sh in (1, 2, 4):
            keep = sub >= sh
            a_sh = jnp.where(keep, pltpu.roll(a, sh, 0), 1.0)
            u_sh = jnp.where(keep, pltpu.roll(u, sh, 0), 0.0)
            u = a * u_sh + u
            a = a * a_sh
        h = u + a * h_prev
        u_scr[rows, :] = h
        return jnp.broadcast_to(h[SUBLANES - 1:SUBLANES, :], (SUBLANES, w))

    carry[...] = lax.fori_loop(0, ts // SUBLANES, body, carry[...])
    y = _head_rms(u_scr[...], gn_ref[...]) * jax.nn.gelu(g_ref[0].astype(F32))
    o_ref[0] = y.astype(o_ref.dtype)


def rg_lru_mixer(proj, conv_w, conv_b, w_rg, b_rg, w_ig, b_ig, lam, gn_b, ts=1024):
    b, s, _ = proj.shape
    w = GROUP_WIDTH
    vec = pl.BlockSpec((1, w), lambda bi, i: (0, 0))
    blockdiag = pl.BlockSpec((N_HEADS, HEAD_DIM, HEAD_DIM), lambda bi, i: (0, 0, 0))
    return pl.pallas_call(
        _lru_kernel,
        grid=(b, s // ts),
        in_specs=[pl.BlockSpec((1, ts, w), lambda bi, i: (bi, i, COL_BX)),
                  pl.BlockSpec((1, ts, w), lambda bi, i: (bi, i, COL_BG)),
                  pl.BlockSpec((CONV_WIDTH, w), lambda bi, i: (0, 0)),
                  vec, blockdiag, vec, blockdiag, vec, vec, vec],
        out_specs=pl.BlockSpec((1, ts, w), lambda bi, i: (bi, i, 0)),
        out_shape=jax.ShapeDtypeStruct((b, s, w), BF16),
        scratch_shapes=[pltpu.VMEM((ts + SUBLANES, w), F32), pltpu.VMEM((ts, w), F32),
                        pltpu.VMEM((ts, w), F32), pltpu.VMEM((SUBLANES, w), F32)],
        compiler_params=_cparams(("arbitrary", "arbitrary")),
        name="rg_lru",
    )(proj, proj, conv_w, conv_b, w_rg, b_rg, w_ig, b_ig, lam, gn_b)


def _bdot(a, b):
    return jnp.einsum("nij,njk->nik", a.astype(BF16), b.astype(BF16), preferred_element_type=F32)


def _bdot_nt(a, b):
    return jnp.einsum("nid,njd->nij", a.astype(BF16), b.astype(BF16), preferred_element_type=F32)


def _bdot_tn(a, b):
    return jnp.einsum("ncd,nce->nde", a.astype(BF16), b.astype(BF16), preferred_element_type=F32)


def _chunk_cumsum(x, c):
    nc = x.shape[0] // c
    tri = jnp.broadcast_to(_tril(c).astype(F32), (nc, c, c))
    return jnp.einsum("nij,njk->nik", tri, x.reshape(nc, c, x.shape[1]), precision=HIGHEST,
                      preferred_element_type=F32)


def _row_forms(x3, specs):
    nc, c, w = x3.shape
    rows = len(specs) * c
    lane = lax.broadcasted_iota(jnp.int32, (rows, w), 1)
    row = lax.broadcasted_iota(jnp.int32, (rows, w), 0)
    sel = jnp.zeros((rows, w), F32)
    for i, spec in enumerate(specs):
        in_rows = (row >= i * c) & (row < (i + 1) * c)
        for ln, coeff in spec:
            sel = jnp.where(in_rows & (lane == ln), coeff, sel)
    out = jnp.einsum("nil,njl->nij", jnp.broadcast_to(sel, (nc, rows, w)), x3, precision=HIGHEST,
                     preferred_element_type=F32)
    return [out[:, i * c:(i + 1) * c, :] for i in range(len(specs))]


def _lane_pick3(x3, lane_idx):
    lane = lax.broadcasted_iota(jnp.int32, x3.shape, 2)
    col = jnp.sum(jnp.where(lane == lane_idx, x3, 0.0), axis=-1, keepdims=True)
    return jnp.broadcast_to(col, x3.shape[:2] + (HEAD_DIM,))


def _gdn_kernel(q_ref, k_ref, v_ref, z_ref, gt_ref, cwq_ref, cwk_ref, cwv_ref, alog_ref, dtb_ref,
                gn_ref, o_ref, qpad, kpad, vpad, p_scr, n_scr, qp_scr, op_scr, gl_scr, o_scr, state):
    ts = q_ref.shape[1]
    c = CHUNK
    hd = HEAD_DIM
    nc = ts // c
    _load_conv_tile(q_ref, qpad, ts)
    _load_conv_tile(k_ref, kpad, ts)
    _load_conv_tile(v_ref, vpad, ts)

    @pl.when(pl.program_id(1) == 0)
    def _():
        state[...] = jnp.zeros(state.shape, F32)

    qc = _silu(_conv_from_pad(qpad, cwq_ref[...], ts))
    kc = _silu(_conv_from_pad(kpad, cwk_ref[...], ts))
    vc = _silu(_conv_from_pad(vpad, cwv_ref[...], ts))
    gates = gt_ref[0]
    neg_a = -jnp.exp(alog_ref[...])
    g_all = neg_a * _softplus(gates + dtb_ref[...])
    beta3_all = _sigmoid(gates).reshape(nc, c, GATE_LANES)
    gc3_all = _chunk_cumsum(g_all, c)
    gc_rows = _row_forms(gc3_all, [[(N_HEADS + h, 1.0)] for h in range(N_HEADS)])
    incl = _tril(c)
    strict = _tril(c, strict=True)
    eye = (lax.broadcasted_iota(jnp.int32, (c, c), 0)
           == lax.broadcasted_iota(jnp.int32, (c, c), 1)).astype(F32)

    q_u, k_u, kb_u, vb_u, kbe_u, qd_u, kd_u, dec_u, gl_u = ([] for _ in range(9))
    for h in range(N_HEADS):
        hs = slice(h * hd, (h + 1) * hd)
        qh = qc[:, hs]
        kh = kc[:, hs]
        qn = (qh * lax.rsqrt(jnp.sum(qh * qh, axis=-1, keepdims=True) + EPS)
              * (hd ** -0.5)).reshape(nc, c, hd)
        kn = (kh * lax.rsqrt(jnp.sum(kh * kh, axis=-1, keepdims=True) + EPS)).reshape(nc, c, hd)
        beta = _lane_pick3(beta3_all, h)
        gc = _lane_pick3(gc3_all, N_HEADS + h)
        e_gc = jnp.exp(gc)
        gc_last = gc[:, c - 1:c, :]
        kb = kn * beta
        q_u.append(qn.astype(BF16))
        k_u.append(kn.astype(BF16))
        kb_u.append(kb.astype(BF16))
        vb_u.append((vc[:, hs].reshape(nc, c, hd) * beta).astype(BF16))
        kbe_u.append((kb * e_gc).astype(BF16))
        qd_u.append(qn * e_gc)
        kd_u.append((kn * jnp.exp(gc_last - gc)).astype(BF16))
        dec_u.append(jnp.exp(jnp.where(incl, gc[:, :, :c] - gc_rows[h], NEG)))
        gl_u.append(jnp.exp(gc_last))
    cat = lambda parts: jnp.concatenate(parts, axis=0)
    q_u, k_u, kb_u, vb_u, kbe_u, qd_u, kd_u, dec_u = (
        cat(t) for t in (q_u, k_u, kb_u, vb_u, kbe_u, qd_u, kd_u, dec_u))
    gl_scr[...] = cat(gl_u)

    a_mat = jnp.where(strict, _bdot_nt(kb_u, k_u) * dec_u, 0.0)
    t_inv = eye - a_mat
    a_pow = a_mat
    for _ in range(int(math.log2(c)) - 1):
        a_pow = _bdot(a_pow, a_pow)
        t_inv = t_inv + _bdot(t_inv, a_pow)
    u = _bdot(t_inv, vb_u)
    w = _bdot(t_inv, kbe_u)
    qk = jnp.where(incl, _bdot_nt(q_u, k_u) * dec_u, 0.0)
    p_scr[...] = _bdot_tn(kd_u, w).astype(BF16)
    n_scr[...] = _bdot_tn(kd_u, u)
    qp_scr[...] = (qd_u - _bdot(qk, w)).astype(BF16)
    op_scr[...] = _bdot(qk, u)

    def step(ci, carry):
        rows = pl.ds(pl.multiple_of(ci * c, c), c)
        for h in range(N_HEADS):
            hs = slice(h * hd, (h + 1) * hd)
            idx = h * nc + ci
            st = state[h]
            stb = st.astype(BF16)
            o_scr[rows, hs] = jnp.dot(qp_scr[idx], stb, preferred_element_type=F32) + op_scr[idx]
            state[h] = (gl_scr[idx] * st - jnp.dot(p_scr[idx], stb, preferred_element_type=F32)
                        + n_scr[idx])
        return carry

    lax.fori_loop(0, nc, step, 0)
    y = _head_rms(o_scr[...], gn_ref[...]) * _silu(z_ref[0].astype(F32))
    o_ref[0] = y.astype(o_ref.dtype)


def gated_delta_mixer(proj, gates, conv_w, a_log_v, dt_bias_v, gn_c, ts=512):
    b, s, _ = proj.shape
    w = GROUP_WIDTH
    hd = HEAD_DIM
    units = N_HEADS * (ts // CHUNK)

    def col(off):
        return pl.BlockSpec((1, ts, w), lambda bi, i, off=off: (bi, i, off))

    def cw(off):
        return pl.BlockSpec((CONV_WIDTH, w), lambda bi, i, off=off: (0, off))

    vec = pl.BlockSpec((1, GATE_LANES), lambda bi, i: (0, 0))
    big = lambda: pltpu.VMEM((ts, w), F32)
    return pl.pallas_call(
        _gdn_kernel,
        grid=(b, s // ts),
        in_specs=[col(COL_CQ), col(COL_CK), col(COL_CV), col(COL_CZ),
                  pl.BlockSpec((1, ts, GATE_LANES), lambda bi, i: (bi, i, 0)),
                  cw(0), cw(1), cw(2), vec, vec,
                  pl.BlockSpec((1, w), lambda bi, i: (0, 0))],
        out_specs=pl.BlockSpec((1, ts, w), lambda bi, i: (bi, i, 0)),
        out_shape=jax.ShapeDtypeStruct((b, s, w), BF16),
        scratch_shapes=[pltpu.VMEM((ts + SUBLANES, w), F32) for _ in range(3)]
        + [pltpu.VMEM((units, hd, hd), BF16), pltpu.VMEM((units, hd, hd), F32),
           pltpu.VMEM((units, CHUNK, hd), BF16), pltpu.VMEM((units, CHUNK, hd), F32),
           pltpu.VMEM((units, 1, hd), F32), big(),
           pltpu.VMEM((N_HEADS, hd, hd), F32)],
        compiler_params=_cparams(("arbitrary", "arbitrary")),
        name="gated_delta_net",
    )(proj, proj, proj, proj, gates, conv_w, conv_w, conv_w, a_log_v, dt_bias_v, gn_c)


def _mlstm_kernel(q_ref, k_ref, v_ref, og_ref, gt_ref, ib_ref, fb_ref, gn_ref, o_ref,
                  kvn_scr, keep_scr, qcn_scr, h_scr, cn_state, m_state):
    ts = q_ref.shape[1]
    c = CHUNK
    hd = HEAD_DIM
    nc = ts // c

    @pl.when(pl.program_id(1) == 0)
    def _():
        cn_state[...] = jnp.zeros(cn_state.shape, F32)
        m_state[...] = jnp.full(m_state.shape, M_INIT, F32)

    gates = gt_ref[0]
    i_all = gates + ib_ref[...]
    f_pre = gates + fb_ref[...]
    lf_all = jnp.minimum(f_pre, 0.0) - jnp.log1p(jnp.exp(-jnp.abs(f_pre)))
    b3_all = _chunk_cumsum(lf_all, c)
    lane = lax.broadcasted_iota(jnp.int32, (nc, c, GATE_LANES), 2)
    ib3_all = jnp.where(lane < 3 * N_HEADS, i_all.reshape(nc, c, GATE_LANES), b3_all)
    ib_rows = _row_forms(ib3_all, [[(2 * N_HEADS + h, 1.0), (3 * N_HEADS + h, -1.0)]
                                   for h in range(N_HEADS)])
    incl = _tril(c)
    k_scale = 1.0 / math.sqrt(hd)
    ones_v = jnp.ones((nc, c, hd), BF16)

    per_head = []
    for h in range(N_HEADS):
        hs = slice(h * hd, (h + 1) * hd)
        q3 = q_ref[0, :, hs].reshape(nc, c, hd)
        k3 = k_ref[0, :, hs].astype(F32).reshape(nc, c, hd) * k_scale
        v3 = v_ref[0, :, hs].reshape(nc, c, hd)
        bb = _lane_pick3(b3_all, 3 * N_HEADS + h)
        ig = _lane_pick3(ib3_all, 2 * N_HEADS + h)
        log_d = jnp.where(incl, bb[:, :, :c] + ib_rows[h], NEG)
        max_ld = jnp.max(log_d, axis=-1, keepdims=True)
        b_last = bb[:, c - 1:c, :]
        log_w = b_last - bb + ig
        max_lw = jnp.max(log_w, axis=1, keepdims=True)
        m = m_state[h]
        m_prev, m_next = [], []
        for ci in range(nc):
            m_prev.append(m)
            m = jnp.maximum(b_last[ci] + m, max_lw[ci])
            m_next.append(m)
        m_state[h] = m
        m_prev = jnp.stack(m_prev, axis=0)
        m_next = jnp.stack(m_next, axis=0)
        inter = bb + m_prev
        m_t = jnp.maximum(inter, max_ld)
        d_mat = jnp.exp(log_d - m_t[:, :, :c])
        e_inter = jnp.exp(inter - m_t)
        sc = _bdot_nt(q3, k3) * d_mat
        scv = _bdot(sc, v3)
        rs = jnp.sum(sc, axis=-1, keepdims=True)
        kw = k3 * jnp.exp(log_w - m_next)
        kvn_scr[h * nc:(h + 1) * nc] = _bdot_tn(kw, jnp.concatenate([v3, ones_v], axis=2))
        keep = jnp.exp(b_last + m_prev - m_next)
        keep_scr[h * nc:(h + 1) * nc] = jnp.concatenate([keep, keep], axis=2)
        per_head.append((e_inter, scv, rs, jnp.exp(-m_t)))

    def step(ci, carry):
        rows = pl.ds(pl.multiple_of(ci * c, c), c)
        for h in range(N_HEADS):
            hs = slice(h * hd, (h + 1) * hd)
            idx = h * nc + ci
            cn = cn_state[h]
            qcn_scr[idx] = _dot(q_ref[0, rows, hs], cn)
            cn_state[h] = keep_scr[idx] * cn + kvn_scr[idx]
        return carry

    lax.fori_loop(0, nc, step, 0)
    for h in range(N_HEADS):
        hs = slice(h * hd, (h + 1) * hd)
        e_inter, scv, rs, e_mt = per_head[h]
        qcn = qcn_scr[h * nc:(h + 1) * nc]
        num = e_inter * qcn[:, :, :hd] + scv
        den = e_inter * qcn[:, :, hd:] + rs
        h_scr[:, hs] = (num / jnp.maximum(jnp.abs(den), e_mt)).reshape(ts, hd)
    y = _head_rms(h_scr[...], gn_ref[...]) * _sigmoid(og_ref[0].astype(F32))
    o_ref[0] = y.astype(o_ref.dtype)


def mlstm_mixer(proj, gates, i_bias_v, f_bias_v, gn_d, ts=512):
    b, s, _ = proj.shape
    w = GROUP_WIDTH
    hd = HEAD_DIM
    units = N_HEADS * (ts // CHUNK)

    def col(off):
        return pl.BlockSpec((1, ts, w), lambda bi, i, off=off: (bi, i, off))

    vec = pl.BlockSpec((1, GATE_LANES), lambda bi, i: (0, 0))
    big = lambda: pltpu.VMEM((ts, w), F32)
    return pl.pallas_call(
        _mlstm_kernel,
        grid=(b, s // ts),
        in_specs=[col(COL_DQ), col(COL_DK), col(COL_DV), col(COL_DO),
                  pl.BlockSpec((1, ts, GATE_LANES), lambda bi, i: (bi, i, 0)),
                  vec, vec, pl.BlockSpec((1, w), lambda bi, i: (0, 0))],
        out_specs=pl.BlockSpec((1, ts, w), lambda bi, i: (bi, i, 0)),
        out_shape=jax.ShapeDtypeStruct((b, s, w), BF16),
        scratch_shapes=[pltpu.VMEM((units, hd, 2 * hd), F32), pltpu.VMEM((units, 1, 2 * hd), F32),
                        pltpu.VMEM((units, CHUNK, 2 * hd), F32), big(),
                        pltpu.VMEM((N_HEADS, hd, 2 * hd), F32),
                        pltpu.VMEM((N_HEADS, 1, hd), F32)],
        compiler_params=_cparams(("arbitrary", "arbitrary")),
        name="mlstm",
    )(proj, proj, proj, proj, gates, i_bias_v, f_bias_v, gn_d)


SLAB = 16
SLAB_LANES = 128
MOE_TM = 256
EXPERT_ROW_GROUP = 64
META_ROWS = 8


def _route(h, wr, rb_col):
    tm = h.shape[0]
    logits = lax.dot_general(wr, h, (((1,), (1,)), ((), ())),
                             precision=HIGHEST, preferred_element_type=F32)
    scores = _sigmoid(logits)
    biased = scores + rb_col
    sr = [scores[e:e + 1, :] for e in range(N_EXPERTS)]
    br = [biased[e:e + 1, :] for e in range(N_EXPERTS)]
    n_groups = N_EXPERTS // EXPERTS_PER_GROUP
    group_scores = []
    for g in range(n_groups):
        a, b, cc, d = br[4 * g:4 * g + 4]
        p, q = jnp.maximum(a, b), jnp.minimum(a, b)
        r, s = jnp.maximum(cc, d), jnp.minimum(cc, d)
        group_scores.append(jnp.maximum(p, r) + jnp.maximum(jnp.minimum(p, r), jnp.maximum(q, s)))
    sel = jnp.zeros((1, tm), jnp.int32)
    best = group_scores[0]
    for g in range(1, n_groups):
        better = group_scores[g] > best
        sel = jnp.where(better, g, sel)
        best = jnp.where(better, group_scores[g], best)
    masked = [jnp.where(sel == (e // EXPERTS_PER_GROUP), br[e], -jnp.inf) for e in range(N_EXPERTS)]

    def top1(vals):
        idx = jnp.zeros((1, tm), jnp.int32)
        bv = vals[0]
        for e in range(1, N_EXPERTS):
            better = vals[e] > bv
            idx = jnp.where(better, e, idx)
            bv = jnp.where(better, vals[e], bv)
        return idx

    i1 = top1(masked)
    i2 = top1([jnp.where(i1 == e, -jnp.inf, masked[e]) for e in range(N_EXPERTS)])
    s1 = sum(jnp.where(i1 == e, sr[e], 0.0) for e in range(N_EXPERTS))
    s2 = sum(jnp.where(i2 == e, sr[e], 0.0) for e in range(N_EXPERTS))
    tot = s1 + s2
    return i1, i2, s1 / tot, s2 / tot


def _router_kernel(x_ref, sc_ref, sh_ref, nw_ref, wr_ref, rb_ref, hs_ref, meta_ref, wts_ref, cnt_ref,
                   run):
    @pl.when((pl.program_id(0) == 0) & (pl.program_id(1) == 0))
    def _():
        run[...] = jnp.zeros(run.shape, F32)

    h = _modulated_norm(x_ref[0], nw_ref[...], sc_ref[0], sh_ref[0])
    tm = h.shape[0]
    for s in range(SLAB):
        hs_ref[pl.ds(s, tm, stride=SLAB), :] = h[:, s * SLAB_LANES:(s + 1) * SLAB_LANES]
    i1, i2, w1, w2 = _route(h, wr_ref[...], rb_ref[...][:, 0:1])
    oh1 = jnp.concatenate([(i1 == e).astype(F32) for e in range(N_EXPERTS)], axis=0)
    oh2 = jnp.concatenate([(i2 == e).astype(F32) for e in range(N_EXPERTS)], axis=0)
    before = (lax.broadcasted_iota(jnp.int32, (tm, tm), 0)
              < lax.broadcasted_iota(jnp.int32, (tm, tm), 1)).astype(BF16)
    excl1 = jnp.dot(oh1.astype(BF16), before, preferred_element_type=F32)
    excl2 = jnp.dot(oh2.astype(BF16), before, preferred_element_type=F32)
    tot1 = jnp.sum(oh1, axis=1, keepdims=True)
    tot2 = jnp.sum(oh2, axis=1, keepdims=True)
    base = run[...][:, 0:1]
    rank1 = jnp.sum(oh1 * (base + excl1), axis=0, keepdims=True)
    rank2 = jnp.sum(oh2 * (base + tot1 + excl2), axis=0, keepdims=True)
    run[...] = run[...] + (tot1 + tot2)
    cnt_ref[...] = run[...]
    meta_ref[...] = jnp.concatenate(
        [i1, i2, rank1.astype(jnp.int32), rank2.astype(jnp.int32),
         jnp.zeros((META_ROWS - 4, tm), jnp.int32)], axis=0)
    wts_ref[...] = jnp.concatenate([w1, w2, jnp.zeros((META_ROWS - 2, tm), F32)], axis=0)


def router(x, scale, shift, norm_w, w_router_t, router_bias_col, tm=512):
    b, s, d = x.shape
    nt = s // tm
    t = b * s
    return pl.pallas_call(
        _router_kernel,
        grid=(b, nt),
        in_specs=[pl.BlockSpec((1, tm, d), lambda bi, i: (bi, i, 0)),
                  pl.BlockSpec((1, 1, d), lambda bi, i: (bi, 0, 0)),
                  pl.BlockSpec((1, 1, d), lambda bi, i: (bi, 0, 0)),
                  pl.BlockSpec((1, d), lambda bi, i: (0, 0)),
                  pl.BlockSpec((N_EXPERTS, d), lambda bi, i: (0, 0)),
                  pl.BlockSpec((N_EXPERTS, GATE_LANES), lambda bi, i: (0, 0))],
        out_specs=[pl.BlockSpec((tm * SLAB, SLAB_LANES), lambda bi, i: (bi * nt + i, 0)),
                   pl.BlockSpec((META_ROWS, tm), lambda bi, i: (0, bi * nt + i)),
                   pl.BlockSpec((META_ROWS, tm), lambda bi, i: (0, bi * nt + i)),
                   pl.BlockSpec((N_EXPERTS, GATE_LANES), lambda bi, i: (0, 0))],
        out_shape=[jax.ShapeDtypeStruct((t * SLAB, SLAB_LANES), F32),
                   jax.ShapeDtypeStruct((META_ROWS, t), jnp.int32),
                   jax.ShapeDtypeStruct((META_ROWS, t), F32),
                   jax.ShapeDtypeStruct((N_EXPERTS, GATE_LANES), F32)],
        scratch_shapes=[pltpu.VMEM((N_EXPERTS, GATE_LANES), F32)],
        compiler_params=_cparams(("arbitrary", "arbitrary")),
        name="router",
    )(x, scale, shift, norm_w, w_router_t, router_bias_col)


def routing_tables(meta, counts, tm=MOE_TM):
    t = meta.shape[1]
    cnt = counts[:, 0].astype(jnp.int32)
    padded = (cnt + tm - 1) // tm * tm
    ends = jnp.cumsum(padded)
    off = ends - padded
    eid = meta[0:2]
    pair_off = sum(jnp.where(eid == e, off[e], 0) for e in range(N_EXPERTS))
    pos = pair_off + meta[2:4]
    n_tiles = 2 * t // tm + N_EXPERTS
    tile_start = jnp.arange(n_tiles, dtype=jnp.int32) * tm
    tile_expert = jnp.minimum(jnp.sum((ends[None, :] <= tile_start[:, None]).astype(jnp.int32), axis=1),
                              N_EXPERTS - 1).astype(jnp.int32)
    n_valid = (ends[-1] // tm).astype(jnp.int32).reshape(1)
    return pos.reshape(-1).astype(jnp.int32), tile_expert, n_valid


def _slab_rows(i):
    return pl.ds(pl.multiple_of(i * SLAB, SLAB), SLAB)


def _dispatch_kernel(pos_ref, hs_ref, xz_ref, xs_ref, pair_ref, sem):
    del xz_ref
    tmd = hs_ref.shape[0] // SLAB
    t = pos_ref.shape[0] // 2
    base = pl.program_id(0) * tmd

    @pl.when(pl.program_id(0) == 0)
    def _():
        def fill(r, carry):
            pair_ref[r] = -1
            return carry

        lax.fori_loop(0, pair_ref.shape[0], fill, 0, unroll=64)

    def issue(r, carry):
        for k in range(2):
            p = pos_ref[k * t + base + r]
            pair_ref[p] = k * t + base + r
            pltpu.make_async_copy(hs_ref.at[_slab_rows(r), :], xs_ref.at[_slab_rows(p), :], sem).start()
        return carry

    lax.fori_loop(0, tmd, issue, 0, unroll=8)
    for k in range(2):
        pltpu.make_async_copy(hs_ref, xs_ref.at[pl.ds(0, tmd * SLAB), :], sem).wait()


def dispatch(pos, hs, xs_init, tmd=512):
    t = hs.shape[0] // SLAB
    return pl.pallas_call(
        _dispatch_kernel,
        grid_spec=pltpu.PrefetchScalarGridSpec(
            num_scalar_prefetch=1,
            grid=(t // tmd,),
            in_specs=[pl.BlockSpec((tmd * SLAB, SLAB_LANES), lambda i, pos: (i, 0)),
                      pl.BlockSpec(memory_space=pl.ANY)],
            out_specs=[pl.BlockSpec(memory_space=pl.ANY), pl.BlockSpec(memory_space=pltpu.SMEM)],
            scratch_shapes=[pltpu.SemaphoreType.DMA]),
        out_shape=[jax.ShapeDtypeStruct(xs_init.shape, F32),
                   jax.ShapeDtypeStruct((xs_init.shape[0] // SLAB,), jnp.int32)],
        input_output_aliases={2: 0},
        compiler_params=_cparams(("arbitrary",)),
        name="moe_dispatch",
    )(pos, hs, xs_init)


def _experts_kernel(te_ref, nv_ref, pair_ref, xs_ref, wg_ref, wu_ref, wd_ref, yi_ref, yt_ref,
                    wgb, wub, wdb, yblk, sem):
    del yi_ref
    j = pl.program_id(0)
    nv = nv_ref[0]
    tm = xs_ref.shape[0] // SLAB

    def wait_rows():
        pltpu.make_async_copy(yblk, yt_ref.at[pl.ds(0, tm * SLAB), :], sem).wait()

    @pl.when(j < nv)
    def _():
        prev = te_ref[jnp.maximum(j - 1, 0)]

        @pl.when((j == 0) | (prev != te_ref[j]))
        def _():
            wgb[...] = wg_ref[0, 0].astype(BF16)
            wub[...] = wu_ref[0, 0].astype(BF16)
            wdb[...] = wd_ref[0, 0].astype(BF16)

        x = jnp.concatenate([xs_ref[pl.ds(s, tm, stride=SLAB), :] for s in range(SLAB)],
                            axis=1).astype(BF16)
        hg = jnp.dot(x, wgb[...], preferred_element_type=F32)
        hu = jnp.dot(x, wub[...], preferred_element_type=F32)
        y = _dot(_silu(hg) * hu, wdb[...])

        @pl.when(j > 0)
        def _():
            wait_rows()

        spare = yt_ref.shape[0] // SLAB - N_EXPERTS * tm + te_ref[j] * tm
        for g0 in range(0, tm, EXPERT_ROW_GROUP):
            for s in range(SLAB):
                yblk[pl.ds(g0 * SLAB + s, EXPERT_ROW_GROUP, stride=SLAB), :] = (
                    y[g0:g0 + EXPERT_ROW_GROUP, s * SLAB_LANES:(s + 1) * SLAB_LANES])
            for r in range(g0, g0 + EXPERT_ROW_GROUP):
                pair = pair_ref[j * tm + r]
                d = jnp.where(pair < 0, spare + r, pair)
                pltpu.make_async_copy(yblk.at[pl.ds(r * SLAB, SLAB), :], yt_ref.at[_slab_rows(d), :],
                                      sem).start()

        @pl.when(j == nv - 1)
        def _():
            wait_rows()


def grouped_experts(tile_expert, n_valid, row_pair, xs, w_gate, w_up, w_down, layer, yt_init,
                    tm=MOE_TM):
    _, ne, d, dff = w_gate.shape
    n_tiles = tile_expert.shape[0]

    def row_tile(j, te, nv, pair):
        return (jnp.minimum(j, nv[0] - 1), 0)

    def expert(j, te, nv, pair):
        return (layer, te[jnp.minimum(j, nv[0] - 1)], 0, 0)

    return pl.pallas_call(
        _experts_kernel,
        grid_spec=pltpu.PrefetchScalarGridSpec(
            num_scalar_prefetch=3,
            grid=(n_tiles,),
            in_specs=[pl.BlockSpec((tm * SLAB, SLAB_LANES), row_tile),
                      pl.BlockSpec((1, 1, d, dff), expert),
                      pl.BlockSpec((1, 1, d, dff), expert),
                      pl.BlockSpec((1, 1, dff, d), expert),
                      pl.BlockSpec(memory_space=pl.ANY)],
            out_specs=pl.BlockSpec(memory_space=pl.ANY),
            scratch_shapes=[pltpu.VMEM((d, dff), BF16), pltpu.VMEM((d, dff), BF16),
                            pltpu.VMEM((dff, d), BF16), pltpu.VMEM((tm * SLAB, SLAB_LANES), F32),
                            pltpu.SemaphoreType.DMA]),
        out_shape=jax.ShapeDtypeStruct(yt_init.shape, F32),
        input_output_aliases={7: 0},
        compiler_params=_cparams(("arbitrary",)),
        name="moe_experts",
    )(tile_expert, n_valid, row_pair, xs, w_gate, w_up, w_down, yt_init)


def _combine_kernel(x_ref, y1_ref, y2_ref, w_ref, gate_ref, nw_ref, o_ref, *, final_norm):
    tm = x_ref.shape[1]
    w = w_ref[...]
    w1 = w[:, 0:1]
    w2 = w[:, 1:2]
    parts = [w1 * y1_ref[pl.ds(s, tm, stride=SLAB), :] + w2 * y2_ref[pl.ds(s, tm, stride=SLAB), :]
             for s in range(SLAB)]
    out = x_ref[0] + gate_ref[0] * jnp.concatenate(parts, axis=1)
    if final_norm:
        ms = jnp.mean(out * out, axis=-1, keepdims=True)
        out = out * lax.rsqrt(ms + EPS) * nw_ref[...]
    o_ref[0] = out


def combine(x, yt, w_cols, gate, norm_w, final_norm, tm=256):
    b, s, d = x.shape
    nt = s // tm
    t = b * s
    return pl.pallas_call(
        functools.partial(_combine_kernel, final_norm=final_norm),
        grid=(b, nt),
        in_specs=[pl.BlockSpec((1, tm, d), lambda bi, i: (bi, i, 0)),
                  pl.BlockSpec((tm * SLAB, SLAB_LANES), lambda bi, i: (bi * nt + i, 0)),
                  pl.BlockSpec((tm * SLAB, SLAB_LANES), lambda bi, i: (t // tm + bi * nt + i, 0)),
                  pl.BlockSpec((tm, META_ROWS), lambda bi, i: (bi * nt + i, 0)),
                  pl.BlockSpec((1, 1, d), lambda bi, i: (bi, 0, 0)),
                  pl.BlockSpec((1, d), lambda bi, i: (0, 0))],
        out_specs=pl.BlockSpec((1, tm, d), lambda bi, i: (bi, i, 0)),
        out_shape=jax.ShapeDtypeStruct((b, s, d), F32),
        compiler_params=_cparams(("arbitrary", "arbitrary")),
        name="moe_combine",
    )(x, yt, yt, w_cols, gate, norm_w)


def _split_w_in(w_in):
    gw = GROUP_WIDTH
    a_end = 3 * gw
    b_end = a_end + 2 * gw
    c_main_end = b_end + 4 * gw
    c_end = c_main_end + 2 * N_HEADS
    d_main_end = c_end + 4 * gw
    main = jnp.concatenate([w_in[:, :c_main_end], w_in[:, c_end:d_main_end]], axis=1)
    small = jnp.concatenate([w_in[:, c_main_end:c_end], w_in[:, d_main_end:]], axis=1)
    small = jnp.pad(small, ((0, 0), (0, GATE_LANES - small.shape[1])))
    return main.astype(BF16), small.astype(BF16)


def _lanes(vec, offset):
    return jnp.zeros((1, GATE_LANES), F32).at[0, offset:offset + N_HEADS].set(vec)


def _rope_tables(s):
    half = HEAD_DIM // 2
    inv_freq = ROPE_THETA ** (-jnp.arange(half, dtype=F32) / half)
    ang = jnp.arange(s, dtype=F32)[:, None] * inv_freq[None, :]
    cos, sin = jnp.cos(ang), jnp.sin(ang)
    return jnp.concatenate([cos, cos], axis=1), jnp.concatenate([-sin, sin], axis=1)


def hybrid_mixer_layer(x, sc1, sh1, g1, norm_w, w_in, w_out, gn_a, conv_b_w, conv_b_b, w_rg, b_rg,
                       w_ig, b_ig, lru_lambda, gn_b, conv_c_w, gdn_a_log, gdn_dt_bias, gdn_norm,
                       mlstm_i_bias, mlstm_f_bias, gn_d, rope):
    w_main, w_small = _split_w_in(w_in)
    proj, gates = in_projection(x, sc1, sh1, norm_w.reshape(1, -1), w_main, w_small)
    row = lambda v: v.reshape(1, -1)
    y_a = dilated_attention_mixer(proj, rope[0], rope[1], row(gn_a))
    y_b = rg_lru_mixer(proj, conv_b_w, row(conv_b_b), w_rg.astype(BF16), row(b_rg),
                       w_ig.astype(BF16), row(b_ig), row(lru_lambda), row(gn_b))
    y_c = gated_delta_mixer(proj, gates, conv_c_w, _lanes(gdn_a_log, N_HEADS),
                            _lanes(gdn_dt_bias, N_HEADS), row(jnp.tile(gdn_norm, N_HEADS)))
    y_d = mlstm_mixer(proj, gates, _lanes(mlstm_i_bias, 2 * N_HEADS),
                      _lanes(mlstm_f_bias, 3 * N_HEADS), row(gn_d))
    return out_projection(x, (y_a, y_b, y_c, y_d), w_out.astype(BF16), g1)


def moe_layer(x, sc2, sh2, g2, norm_w, w_router, router_bias, w_gate, w_up, w_down, layer, xs_buf,
              yt_buf, final_norm_w, apply_final_norm):
    rb = jnp.broadcast_to(router_bias.reshape(-1, 1), (N_EXPERTS, GATE_LANES))
    hs, meta, wts, counts = router(x, sc2, sh2, norm_w.reshape(1, -1), w_router.T, rb)
    pos, tile_expert, n_valid = routing_tables(meta, counts)
    xs, row_pair = dispatch(pos, hs, xs_buf)
    yt = grouped_experts(tile_expert, n_valid, row_pair, xs, w_gate, w_up, w_down, layer, yt_buf)
    return combine(x, yt, wts.T, g2, final_norm_w, apply_final_norm), xs, yt


def moe_buffers(n_tokens):
    rows = 2 * n_tokens + N_EXPERTS * MOE_TM
    return jnp.zeros((rows * SLAB, SLAB_LANES), F32), jnp.zeros((rows * SLAB, SLAB_LANES), F32)


def kernel(x, c, norm_mix, norm_ffn, norm_final, w_ada, b_ada, w_in, w_out, gn_a, conv_b_w, conv_b_b, w_rg, b_rg, w_ig, b_ig, lru_lambda, gn_b, conv_c_w, gdn_a_log, gdn_dt_bias, gdn_norm, mlstm_i_bias, mlstm_f_bias, gn_d, w_router, router_bias, moe_w_gate, moe_w_up, moe_w_down):
    depth = w_ada.shape[0]
    b, s, d = x.shape
    mod = ada_modulation(c, w_ada, b_ada)
    rope = _rope_tables(s)
    xs_buf, yt_buf = moe_buffers(b * s)
    for l in range(depth):
        sh1, sc1, g1, sh2, sc2, g2 = [mod[l, :, None, i * d:(i + 1) * d] for i in range(6)]
        x = hybrid_mixer_layer(x, sc1, sh1, g1, norm_mix[l], w_in[l], w_out[l], gn_a[l],
                               conv_b_w[l], conv_b_b[l], w_rg[l], b_rg[l], w_ig[l], b_ig[l],
                               lru_lambda[l], gn_b[l], conv_c_w[l], gdn_a_log[l], gdn_dt_bias[l],
                               gdn_norm[l], mlstm_i_bias[l], mlstm_f_bias[l], gn_d[l], rope)
        x, xs_buf, yt_buf = moe_layer(x, sc2, sh2, g2, norm_ffn[l], w_router, router_bias,
                                      moe_w_gate, moe_w_up, moe_w_down, l, xs_buf, yt_buf,
                                      norm_final.reshape(1, -1), l == depth - 1)
    return x
```

```python
import functools
import math

import jax
import jax.numpy as jnp
from jax import lax
from jax.experimental import pallas as pl
from jax.experimental.pallas import tpu as pltpu

F32 = jnp.float32
BF16 = jnp.bfloat16
HIGHEST = lax.Precision.HIGHEST

HEAD_DIM = 128
N_HEADS = 4
GROUP_WIDTH = N_HEADS * HEAD_DIM
DILATED_PAIRS = ((128, 1), (512, 4), (2048, 16))
ATTN_BLOCK = 128
ROPE_THETA = 10000.0
CONV_WIDTH = 4
LRU_C = 8.0
CHUNK = 64
N_EXPERTS = 16
EXPERTS_PER_GROUP = 4
EPS = 1e-6
M_INIT = -1e30
NEG = -1e30
SUBLANES = 8
VMEM_LIMIT = 56 * 1024 * 1024

COL_AQ, COL_AK, COL_AV = 0, 4, 8
COL_BX, COL_BG = 3, 4
COL_CQ, COL_CK, COL_CV, COL_CZ = 5, 6, 7, 8
COL_DQ, COL_DK, COL_DV, COL_DO = 9, 10, 11, 12
MAIN_COLS = 13 * GROUP_WIDTH
GATE_LANES = 128


def _cparams(sem):
    return pltpu.CompilerParams(dimension_semantics=sem, vmem_limit_bytes=VMEM_LIMIT)


def _dot(a, b):
    return jnp.dot(a.astype(BF16), b.astype(BF16), preferred_element_type=F32)


def _dot_nt(a, b):
    return lax.dot_general(a.astype(BF16), b.astype(BF16), (((1,), (1,)), ((), ())),
                           preferred_element_type=F32)


def _softplus(z):
    return jnp.maximum(z, 0.0) + jnp.log1p(jnp.exp(-jnp.abs(z)))


def _sigmoid(z):
    return 1.0 / (1.0 + jnp.exp(-z))


def _silu(z):
    return z * _sigmoid(z)


def _head_rms(x, gain):
    outs = []
    for g in range(x.shape[1] // HEAD_DIM):
        xg = x[:, g * HEAD_DIM:(g + 1) * HEAD_DIM]
        ms = jnp.mean(xg * xg, axis=-1, keepdims=True)
        outs.append(xg * lax.rsqrt(ms + EPS))
    y = outs[0] if len(outs) == 1 else jnp.concatenate(outs, axis=1)
    return y * gain


def _tril(c, strict=False):
    row = lax.broadcasted_iota(jnp.int32, (c, c), 0)
    col = lax.broadcasted_iota(jnp.int32, (c, c), 1)
    return (col < row) if strict else (col <= row)


def _ada_kernel(c_ref, w_ref, b_ref, o_ref):
    c = c_ref[...]
    o_ref[0] = _dot(_silu(c), w_ref[0]) + b_ref[0]


def ada_modulation(c, w_ada, b_ada):
    depth, d, n = w_ada.shape
    b = c.shape[0]
    rows = -(-b // SUBLANES) * SUBLANES
    c_pad = jnp.zeros((rows, d), F32).at[:b].set(c)
    tn = 1024
    out = pl.pallas_call(
        _ada_kernel,
        grid=(depth, n // tn),
        in_specs=[pl.BlockSpec((rows, d), lambda l, j: (0, 0)),
                  pl.BlockSpec((1, d, tn), lambda l, j: (l, 0, j)),
                  pl.BlockSpec((1, 1, tn), lambda l, j: (l, 0, j))],
        out_specs=pl.BlockSpec((1, rows, tn), lambda l, j: (l, 0, j)),
        out_shape=jax.ShapeDtypeStruct((depth, rows, n), F32),
        compiler_params=_cparams(("arbitrary", "arbitrary")),
        name="ada_modulation",
    )(c_pad, w_ada, b_ada.reshape(depth, 1, n))
    return out[:, :b]


def _modulated_norm(x, nw, sc, sh):
    ms = jnp.mean(x * x, axis=-1, keepdims=True)
    return (x * lax.rsqrt(ms + EPS) * nw) * (1.0 + sc) + sh


def _in_proj_kernel(x_ref, sc_ref, sh_ref, nw_ref, w_ref, wg_ref, proj_ref, gates_ref, h_scr):
    @pl.when(pl.program_id(2) == 0)
    def _():
        h = _modulated_norm(x_ref[0], nw_ref[...], sc_ref[0], sh_ref[0]).astype(BF16)
        h_scr[...] = h
        gates_ref[0] = jnp.dot(h, wg_ref[...], preferred_element_type=F32)

    proj_ref[0] = jnp.dot(h_scr[...], w_ref[...], preferred_element_type=F32).astype(BF16)


def in_projection(x, scale, shift, norm_w, w_main, w_gate, tm=512, tn=1664):
    b, s, d = x.shape
    n = w_main.shape[1]
    return pl.pallas_call(
        _in_proj_kernel,
        grid=(b, s // tm, n // tn),
        in_specs=[pl.BlockSpec((1, tm, d), lambda bi, i, j: (bi, i, 0)),
                  pl.BlockSpec((1, 1, d), lambda bi, i, j: (bi, 0, 0)),
                  pl.BlockSpec((1, 1, d), lambda bi, i, j: (bi, 0, 0)),
                  pl.BlockSpec((1, d), lambda bi, i, j: (0, 0)),
                  pl.BlockSpec((d, tn), lambda bi, i, j: (0, j)),
                  pl.BlockSpec((d, GATE_LANES), lambda bi, i, j: (0, 0))],
        out_specs=[pl.BlockSpec((1, tm, tn), lambda bi, i, j: (bi, i, j)),
                   pl.BlockSpec((1, tm, GATE_LANES), lambda bi, i, j: (bi, i, 0))],
        out_shape=[jax.ShapeDtypeStruct((b, s, n), BF16),
                   jax.ShapeDtypeStruct((b, s, GATE_LANES), F32)],
        scratch_shapes=[pltpu.VMEM((tm, d), BF16)],
        compiler_params=_cparams(("arbitrary", "arbitrary", "arbitrary")),
        name="in_projection",
    )(x, scale, shift, norm_w, w_main, w_gate)


def _out_proj_kernel(x_ref, ya_ref, yb_ref, yc_ref, yd_ref, w_ref, g_ref, o_ref):
    acc = None
    for k, y_ref in enumerate((ya_ref, yb_ref, yc_ref, yd_ref)):
        part = jnp.dot(y_ref[0], w_ref[k * GROUP_WIDTH:(k + 1) * GROUP_WIDTH, :],
                       preferred_element_type=F32)
        acc = part if acc is None else acc + part
    o_ref[0] = x_ref[0] + g_ref[0] * acc


def out_projection(x, ys, w_out, gate, tm=512):
    b, s, d = x.shape
    yspec = pl.BlockSpec((1, tm, GROUP_WIDTH), lambda bi, i: (bi, i, 0))
    return pl.pallas_call(
        _out_proj_kernel,
        grid=(b, s // tm),
        in_specs=[pl.BlockSpec((1, tm, d), lambda bi, i: (bi, i, 0)),
                  yspec, yspec, yspec, yspec,
                  pl.BlockSpec(w_out.shape, lambda bi, i: (0, 0)),
                  pl.BlockSpec((1, 1, d), lambda bi, i: (bi, 0, 0))],
        out_specs=pl.BlockSpec((1, tm, d), lambda bi, i: (bi, i, 0)),
        out_shape=jax.ShapeDtypeStruct((b, s, d), F32),
        compiler_params=_cparams(("arbitrary", "arbitrary")),
        name="out_projection",
    )(x, *ys, w_out, gate)


def _attn_kernel(q_ref, k_ref, v_ref, cos_ref, sin_ref, gn_ref, o_ref, qs, ks, vs, acc, ms, ls,
                 q4, k4, v4):
    s = q_ref.shape[1]
    blk = ATTN_BLOCK
    cos = cos_ref[...]
    sin = sin_ref[...]

    def rot(t):
        return t * cos + pltpu.roll(t, HEAD_DIM // 2, 1) * sin

    qs[...] = rot(q_ref[0].astype(F32)) * (1.0 / math.sqrt(HEAD_DIM))
    ks[...] = rot(k_ref[0].astype(F32))
    vs[...] = v_ref[0].astype(F32)

    row = lax.broadcasted_iota(jnp.int32, (blk, 2 * blk), 0)
    col = lax.broadcasted_iota(jnp.int32, (blk, 2 * blk), 1)
    own_ok = (col >= blk) & (col - blk <= row)

    def run_branch(refs, stride, span, locate, first_branch):
        q_s, k_s, v_s, acc_s, m_s, l_s = refs

        def rows(start):
            if stride == 1:
                return pl.ds(pl.multiple_of(start, blk), blk)
            return pl.ds(start, blk, stride=stride)

        def body(idx, carry):
            start, has_prev = locate(idx)
            pstart = jnp.where(has_prev, start - span, start)
            qb = q_s[rows(start), :]
            kc = jnp.concatenate([k_s[rows(pstart), :], k_s[rows(start), :]], axis=0)
            vc = jnp.concatenate([v_s[rows(pstart), :], v_s[rows(start), :]], axis=0)
            sc = _dot_nt(qb, kc)
            first = jnp.where(has_prev, 0, 4 * blk)
            valid = own_ok | ((col < blk) & (col >= row + first))
            sc = jnp.where(valid, sc, NEG)
            mb = jnp.max(sc, axis=-1, keepdims=True)
            p = jnp.exp(sc - mb)
            den = jnp.sum(p, axis=-1, keepdims=True)
            num = _dot(p, vc)
            mb = jnp.broadcast_to(mb, (blk, HEAD_DIM))
            den = jnp.broadcast_to(den, (blk, HEAD_DIM))
            if first_branch:
                acc_s[rows(start), :] = num
                m_s[rows(start), :] = mb
                l_s[rows(start), :] = den
            else:
                m_old = m_s[rows(start), :]
                m_new = jnp.maximum(m_old, mb)
                a_old = jnp.exp(m_old - m_new)
                a_blk = jnp.exp(mb - m_new)
                acc_s[rows(start), :] = acc_s[rows(start), :] * a_old + num * a_blk
                l_s[rows(start), :] = l_s[rows(start), :] * a_old + den * a_blk
                m_s[rows(start), :] = m_new
            return carry

        lax.fori_loop(0, s // blk, body, 0, unroll=8)

    def natural(dil):
        nb = s // (blk * dil)

        def locate(idx):
            n = idx % nb
            return idx // nb + n * (blk * dil), n > 0

        return locate

    (_, d0), (_, d1), (_, d2) = DILATED_PAIRS
    run_branch((qs, ks, vs, acc, ms, ls), d0, blk * d0, natural(d0), True)
    run_branch((qs, ks, vs, acc, ms, ls), d1, blk * d1, natural(d1), False)

    fold = 4
    seg = s // fold
    sub = d2 // fold

    def deinterleave(src, dst):
        for r1 in range(fold):
            dst[r1 * seg:(r1 + 1) * seg, :] = src[pl.ds(r1, seg, stride=fold), :]

    deinterleave(qs, q4)
    deinterleave(ks, k4)
    deinterleave(vs, v4)
    deinterleave(acc, qs)
    deinterleave(ms, ks)
    deinterleave(ls, vs)
    nb2 = s // (blk * d2)

    def folded(idx):
        r = idx // nb2
        n = idx % nb2
        return (r % fold) * seg + r // fold + n * (blk * sub), n > 0

    run_branch((q4, k4, v4, qs, ks, vs), sub, blk * sub, folded, False)
    y4 = _head_rms(qs[...] / vs[...], gn_ref[...])
    for r1 in range(fold):
        acc[pl.ds(r1, seg, stride=fold), :] = y4[r1 * seg:(r1 + 1) * seg, :]
    o_ref[0] = acc[...].astype(o_ref.dtype)


def dilated_attention_mixer(proj, cos2, sin2, gn_a):
    b, s, _ = proj.shape
    hd = HEAD_DIM

    def col(off):
        return pl.BlockSpec((1, s, hd), lambda bi, h, off=off: (bi, 0, off + h))

    return pl.pallas_call(
        _attn_kernel,
        grid=(b, N_HEADS),
        in_specs=[col(COL_AQ), col(COL_AK), col(COL_AV),
                  pl.BlockSpec((s, hd), lambda bi, h: (0, 0)),
                  pl.BlockSpec((s, hd), lambda bi, h: (0, 0)),
                  pl.BlockSpec((1, hd), lambda bi, h: (0, h))],
        out_specs=pl.BlockSpec((1, s, hd), lambda bi, h: (bi, 0, h)),
        out_shape=jax.ShapeDtypeStruct((b, s, GROUP_WIDTH), BF16),
        scratch_shapes=[pltpu.VMEM((s, hd), F32) for _ in range(9)],
        compiler_params=_cparams(("arbitrary", "arbitrary")),
        name="dilated_attention",
    )(proj, proj, proj, cos2, sin2, gn_a)


def _load_conv_tile(x_ref, xpad, ts):
    @pl.when(pl.program_id(1) == 0)
    def _():
        xpad[0:SUBLANES, :] = jnp.zeros((SUBLANES, xpad.shape[1]), F32)

    @pl.when(pl.program_id(1) > 0)
    def _():
        xpad[0:SUBLANES, :] = xpad[ts:ts + SUBLANES, :]

    xpad[SUBLANES:SUBLANES + ts, :] = x_ref[0].astype(F32)


def _conv_from_pad(xpad, w, ts):
    y = None
    for j in range(CONV_WIDTH):
        off = SUBLANES - (CONV_WIDTH - 1) + j
        term = xpad[off:off + ts, :] * w[j:j + 1, :]
        y = term if y is None else y + term
    return y


def _lru_kernel(x_ref, g_ref, cw_ref, cb_ref, wr_ref, br_ref, wi_ref, bi_ref, lam_ref, gn_ref,
                o_ref, xpad, a_scr, u_scr, carry):
    ts = x_ref.shape[1]
    w = x_ref.shape[2]
    _load_conv_tile(x_ref, xpad, ts)

    @pl.when(pl.program_id(1) == 0)
    def _():
        carry[...] = jnp.zeros(carry.shape, F32)

    xc = _conv_from_pad(xpad, cw_ref[...], ts) + cb_ref[...]
    r_parts, i_parts = [], []
    for g in range(N_HEADS):
        xg = xc[:, g * HEAD_DIM:(g + 1) * HEAD_DIM]
        r_parts.append(_dot(xg, wr_ref[g]))
        i_parts.append(_dot(xg, wi_ref[g]))
    r = _sigmoid(jnp.concatenate(r_parts, axis=1) + br_ref[...])
    ig = _sigmoid(jnp.concatenate(i_parts, axis=1) + bi_ref[...])
    log_a = (-LRU_C) * r * _softplus(-lam_ref[...])
    a_scr[...] = jnp.exp(log_a)
    u_scr[...] = jnp.sqrt(1.0 - jnp.exp(2.0 * log_a)) * (ig * xc)

    sub = lax.broadcasted_iota(jnp.int32, (SUBLANES, w), 0)

    def body(j, h_prev):
        rows = pl.ds(pl.multiple_of(j * SUBLANES, SUBLANES), SUBLANES)
        a = a_scr[rows, :]
        u = u_scr[rows, :]
        for sh in (1, 2, 4):
            keep = sub >= sh
            a_sh = jnp.where(keep, pltpu.roll(a, sh, 0), 1.0)
            u_sh = jnp.where(keep, pltpu.roll(u, sh, 0), 0.0)
            u = a * u_sh + u
            a = a * a_sh
        h = u + a * h_prev
        u_scr[rows, :] = h
        return jnp.broadcast_to(h[SUBLANES - 1:SUBLANES, :], (SUBLANES, w))

    carry[...] = lax.fori_loop(0, ts // SUBLANES, body, carry[...])
    y = _head_rms(u_scr[...], gn_ref[...]) * jax.nn.gelu(g_ref[0].astype(F32))
    o_ref[0] = y.astype(o_ref.dtype)


def rg_lru_mixer(proj, conv_w, conv_b, w_rg, b_rg, w_ig, b_ig, lam, gn_b, ts=1024):
    b, s, _ = proj.shape
    w = GROUP_WIDTH
    vec = pl.BlockSpec((1, w), lambda bi, i: (0, 0))
    blockdiag = pl.BlockSpec((N_HEADS, HEAD_DIM, HEAD_DIM), lambda bi, i: (0, 0, 0))
    return pl.pallas_call(
        _lru_kernel,
        grid=(b, s // ts),
        in_specs=[pl.BlockSpec((1, ts, w), lambda bi, i: (bi, i, COL_BX)),
                  pl.BlockSpec((1, ts, w), lambda bi, i: (bi, i, COL_BG)),
                  pl.BlockSpec((CONV_WIDTH, w), lambda bi, i: (0, 0)),
                  vec, blockdiag, vec, blockdiag, vec, vec, vec],
        out_specs=pl.BlockSpec((1, ts, w), lambda bi, i: (bi, i, 0)),
        out_shape=jax.ShapeDtypeStruct((b, s, w), BF16),
        scratch_shapes=[pltpu.VMEM((ts + SUBLANES, w), F32), pltpu.VMEM((ts, w), F32),
                        pltpu.VMEM((ts, w), F32), pltpu.VMEM((SUBLANES, w), F32)],
        compiler_params=_cparams(("arbitrary", "arbitrary")),
        name="rg_lru",
    )(proj, proj, conv_w, conv_b, w_rg, b_rg, w_ig, b_ig, lam, gn_b)


def _bdot(a, b):
    return jnp.einsum("nij,njk->nik", a.astype(BF16), b.astype(BF16), preferred_element_type=F32)


def _bdot_nt(a, b):
    return jnp.einsum("nid,njd->nij", a.astype(BF16), b.astype(BF16), preferred_element_type=F32)


def _bdot_tn(a, b):
    return jnp.einsum("ncd,nce->nde", a.astype(BF16), b.astype(BF16), preferred_element_type=F32)


def _chunk_cumsum(x, c):
    nc = x.shape[0] // c
    tri = jnp.broadcast_to(_tril(c).astype(F32), (nc, c, c))
    return jnp.einsum("nij,njk->nik", tri, x.reshape(nc, c, x.shape[1]), precision=HIGHEST,
                      preferred_element_type=F32)


def _row_forms(x3, specs):
    nc, c, w = x3.shape
    rows = len(specs) * c
    lane = lax.broadcasted_iota(jnp.int32, (rows, w), 1)
    row = lax.broadcasted_iota(jnp.int32, (rows, w), 0)
    sel = jnp.zeros((rows, w), F32)
    for i, spec in enumerate(specs):
        in_rows = (row >= i * c) & (row < (i + 1) * c)
        for ln, coeff in spec:
            sel = jnp.where(in_rows & (lane == ln), coeff, sel)
    out = jnp.einsum("nil,njl->nij", jnp.broadcast_to(sel, (nc, rows, w)), x3, precision=HIGHEST,
                     preferred_element_type=F32)
    return [out[:, i * c:(i + 1) * c, :] for i in range(len(specs))]


def _lane_pick3(x3, lane_idx):
    lane = lax.broadcasted_iota(jnp.int32, x3.shape, 2)
    col = jnp.sum(jnp.where(lane == lane_idx, x3, 0.0), axis=-1, keepdims=True)
    return jnp.broadcast_to(col, x3.shape[:2] + (HEAD_DIM,))


def _gdn_kernel(q_ref, k_ref, v_ref, z_ref, gt_ref, cwq_ref, cwk_ref, cwv_ref, alog_ref, dtb_ref,
                gn_ref, o_ref, qpad, kpad, vpad, p_scr, n_scr, qp_scr, op_scr, gl_scr, o_scr, state):
    ts = q_ref.shape[1]
    c = CHUNK
    hd = HEAD_DIM
    nc = ts // c
    _load_conv_tile(q_ref, qpad, ts)
    _load_conv_tile(k_ref, kpad, ts)
    _load_conv_tile(v_ref, vpad, ts)

    @pl.when(pl.program_id(1) == 0)
    def _():
        state[...] = jnp.zeros(state.shape, F32)

    qc = _silu(_conv_from_pad(qpad, cwq_ref[...], ts))
    kc = _silu(_conv_from_pad(kpad, cwk_ref[...], ts))
    vc = _silu(_conv_from_pad(vpad, cwv_ref[...], ts))
    gates = gt_ref[0]
    neg_a = -jnp.exp(alog_ref[...])
    g_all = neg_a * _softplus(gates + dtb_ref[...])
    beta3_all = _sigmoid(gates).reshape(nc, c, GATE_LANES)
    gc3_all = _chunk_cumsum(g_all, c)
    gc_rows = _row_forms(gc3_all, [[(N_HEADS + h, 1.0)] for h in range(N_HEADS)])
    incl = _tril(c)
    strict = _tril(c, strict=True)
    eye = (lax.broadcasted_iota(jnp.int32, (c, c), 0)
           == lax.broadcasted_iota(jnp.int32, (c, c), 1)).astype(F32)

    q_u, k_u, kb_u, vb_u, kbe_u, qd_u, kd_u, dec_u, gl_u = ([] for _ in range(9))
    for h in range(N_HEADS):
        hs = slice(h * hd, (h + 1) * hd)
        qh = qc[:, hs]
        kh = kc[:, hs]
        qn = (qh * lax.rsqrt(jnp.sum(qh * qh, axis=-1, keepdims=True) + EPS)
              * (hd ** -0.5)).reshape(nc, c, hd)
        kn = (kh * lax.rsqrt(jnp.sum(kh * kh, axis=-1, keepdims=True) + EPS)).reshape(nc, c, hd)
        beta = _lane_pick3(beta3_all, h)
        gc = _lane_pick3(gc3_all, N_HEADS + h)
        e_gc = jnp.exp(gc)
        gc_last = gc[:, c - 1:c, :]
        kb = kn * beta
        q_u.append(qn.astype(BF16))
        k_u.append(kn.astype(BF16))
        kb_u.append(kb.astype(BF16))
        vb_u.append((vc[:, hs].reshape(nc, c, hd) * beta).astype(BF16))
        kbe_u.append((kb * e_gc).astype(BF16))
        qd_u.append(qn * e_gc)
        kd_u.append((kn * jnp.exp(gc_last - gc)).astype(BF16))
        dec_u.append(jnp.exp(jnp.where(incl, gc[:, :, :c] - gc_rows[h], NEG)))
        gl_u.append(jnp.exp(gc_last))
    cat = lambda parts: jnp.concatenate(parts, axis=0)
    q_u, k_u, kb_u, vb_u, kbe_u, qd_u, kd_u, dec_u = (
        cat(t) for t in (q_u, k_u, kb_u, vb_u, kbe_u, qd_u, kd_u, dec_u))
    gl_scr[...] = cat(gl_u)

    a_mat = jnp.where(strict, _bdot_nt(kb_u, k_u) * dec_u, 0.0)
    t_inv = eye - a_mat
    a_pow = a_mat
    for _ in range(int(math.log2(c)) - 1):
        a_pow = _bdot(a_pow, a_pow)
        t_inv = t_inv + _bdot(t_inv, a_pow)
    u = _bdot(t_inv, vb_u)
    w = _bdot(t_inv, kbe_u)
    qk = jnp.where(incl, _bdot_nt(q_u, k_u) * dec_u, 0.0)
    p_scr[...] = _bdot_tn(kd_u, w).astype(BF16)
    n_scr[...] = _bdot_tn(kd_u, u)
    qp_scr[...] = (qd_u - _bdot(qk, w)).astype(BF16)
    op_scr[...] = _bdot(qk, u)

    def step(ci, carry):
        rows = pl.ds(pl.multiple_of(ci * c, c), c)
        for h in range(N_HEADS):
            hs = slice(h * hd, (h + 1) * hd)
            idx = h * nc + ci
            st = state[h]
            stb = st.astype(BF16)
            o_scr[rows, hs] = jnp.dot(qp_scr[idx], stb, preferred_element_type=F32) + op_scr[idx]
            state[h] = (gl_scr[idx] * st - jnp.dot(p_scr[idx], stb, preferred_element_type=F32)
                        + n_scr[idx])
        return carry

    lax.fori_loop(0, nc, step, 0)
    y = _head_rms(o_scr[...], gn_ref[...]) * _silu(z_ref[0].astype(F32))
    o_ref[0] = y.astype(o_ref.dtype)


def gated_delta_mixer(proj, gates, conv_w, a_log_v, dt_bias_v, gn_c, ts=512):
    b, s, _ = proj.shape
    w = GROUP_WIDTH
    hd = HEAD_DIM
    units = N_HEADS * (ts // CHUNK)

    def col(off):
        return pl.BlockSpec((1, ts, w), lambda bi, i, off=off: (bi, i, off))

    def cw(off):
        return pl.BlockSpec((CONV_WIDTH, w), lambda bi, i, off=off: (0, off))

    vec = pl.BlockSpec((1, GATE_LANES), lambda bi, i: (0, 0))
    big = lambda: pltpu.VMEM((ts, w), F32)
    return pl.pallas_call(
        _gdn_kernel,
        grid=(b, s // ts),
        in_specs=[col(COL_CQ), col(COL_CK), col(COL_CV), col(COL_CZ),
                  pl.BlockSpec((1, ts, GATE_LANES), lambda bi, i: (bi, i, 0)),
                  cw(0), cw(1), cw(2), vec, vec,
                  pl.BlockSpec((1, w), lambda bi, i: (0, 0))],
        out_specs=pl.BlockSpec((1, ts, w), lambda bi, i: (bi, i, 0)),
        out_shape=jax.ShapeDtypeStruct((b, s, w), BF16),
        scratch_shapes=[pltpu.VMEM((ts + SUBLANES, w), F32) for _ in range(3)]
        + [pltpu.VMEM((units, hd, hd), BF16), pltpu.VMEM((units, hd, hd), F32),
           pltpu.VMEM((units, CHUNK, hd), BF16), pltpu.VMEM((units, CHUNK, hd), F32),
           pltpu.VMEM((units, 1, hd), F32), big(),
           pltpu.VMEM((N_HEADS, hd, hd), F32)],
        compiler_params=_cparams(("arbitrary", "arbitrary")),
        name="gated_delta_net",
    )(proj, proj, proj, proj, gates, conv_w, conv_w, conv_w, a_log_v, dt_bias_v, gn_c)


def _mlstm_kernel(q_ref, k_ref, v_ref, og_ref, gt_ref, ib_ref, fb_ref, gn_ref, o_ref,
                  kvn_scr, keep_scr, qcn_scr, h_scr, cn_state, m_state):
    ts = q_ref.shape[1]
    c = CHUNK
    hd = HEAD_DIM
    nc = ts // c

    @pl.when(pl.program_id(1) == 0)
    def _():
        cn_state[...] = jnp.zeros(cn_state.shape, F32)
        m_state[...] = jnp.full(m_state.shape, M_INIT, F32)

    gates = gt_ref[0]
    i_all = gates + ib_ref[...]
    f_pre = gates + fb_ref[...]
    lf_all = jnp.minimum(f_pre, 0.0) - jnp.log1p(jnp.exp(-jnp.abs(f_pre)))
    b3_all = _chunk_cumsum(lf_all, c)
    lane = lax.broadcasted_iota(jnp.int32, (nc, c, GATE_LANES), 2)
    ib3_all = jnp.where(lane < 3 * N_HEADS, i_all.reshape(nc, c, GATE_LANES), b3_all)
    ib_rows = _row_forms(ib3_all, [[(2 * N_HEADS + h, 1.0), (3 * N_HEADS + h, -1.0)]
                                   for h in range(N_HEADS)])
    incl = _tril(c)
    k_scale = 1.0 / math.sqrt(hd)
    ones_v = jnp.ones((nc, c, hd), BF16)

    per_head = []
    for h in range(N_HEADS):
        hs = slice(h * hd, (h + 1) * hd)
        q3 = q_ref[0, :, hs].reshape(nc, c, hd)
        k3 = k_ref[0, :, hs].astype(F32).reshape(nc, c, hd) * k_scale
        v3 = v_ref[0, :, hs].reshape(nc, c, hd)
        bb = _lane_pick3(b3_all, 3 * N_HEADS + h)
        ig = _lane_pick3(ib3_all, 2 * N_HEADS + h)
        log_d = jnp.where(incl, bb[:, :, :c] + ib_rows[h], NEG)
        max_ld = jnp.max(log_d, axis=-1, keepdims=True)
        b_last = bb[:, c - 1:c, :]
        log_w = b_last - bb + ig
        max_lw = jnp.max(log_w, axis=1, keepdims=True)
        m = m_state[h]
        m_prev, m_next = [], []
        for ci in range(nc):
            m_prev.append(m)
            m = jnp.maximum(b_last[ci] + m, max_lw[ci])
            m_next.append(m)
        m_state[h] = m
        m_prev = jnp.stack(m_prev, axis=0)
        m_next = jnp.stack(m_next, axis=0)
        inter = bb + m_prev
        m_t = jnp.maximum(inter, max_ld)
        d_mat = jnp.exp(log_d - m_t[:, :, :c])
        e_inter = jnp.exp(inter - m_t)
        sc = _bdot_nt(q3, k3) * d_mat
        scv = _bdot(sc, v3)
        rs = jnp.sum(sc, axis=-1, keepdims=True)
        kw = k3 * jnp.exp(log_w - m_next)
        kvn_scr[h * nc:(h + 1) * nc] = _bdot_tn(kw, jnp.concatenate([v3, ones_v], axis=2))
        keep = jnp.exp(b_last + m_prev - m_next)
        keep_scr[h * nc:(h + 1) * nc] = jnp.concatenate([keep, keep], axis=2)
        per_head.append((e_inter, scv, rs, jnp.exp(-m_t)))

    def step(ci, carry):
        rows = pl.ds(pl.multiple_of(ci * c, c), c)
        for h in range(N_HEADS):
            hs = slice(h * hd, (h + 1) * hd)
            idx = h * nc + ci
            cn = cn_state[h]
            qcn_scr[idx] = _dot(q_ref[0, rows, hs], cn)
            cn_state[h] = keep_scr[idx] * cn + kvn_scr[idx]
        return carry

    lax.fori_loop(0, nc, step, 0)
    for h in range(N_HEADS):
        hs = slice(h * hd, (h + 1) * hd)
        e_inter, scv, rs, e_mt = per_head[h]
        qcn = qcn_scr[h * nc:(h + 1) * nc]
        num = e_inter * qcn[:, :, :hd] + scv
        den = e_inter * qcn[:, :, hd:] + rs
        h_scr[:, hs] = (num / jnp.maximum(jnp.abs(den), e_mt)).reshape(ts, hd)
    y = _head_rms(h_scr[...], gn_ref[...]) * _sigmoid(og_ref[0].astype(F32))
    o_ref[0] = y.astype(o_ref.dtype)


def mlstm_mixer(proj, gates, i_bias_v, f_bias_v, gn_d, ts=512):
    b, s, _ = proj.shape
    w = GROUP_WIDTH
    hd = HEAD_DIM
    units = N_HEADS * (ts // CHUNK)

    def col(off):
        return pl.BlockSpec((1, ts, w), lambda bi, i, off=off: (bi, i, off))

    vec = pl.BlockSpec((1, GATE_LANES), lambda bi, i: (0, 0))
    big = lambda: pltpu.VMEM((ts, w), F32)
    return pl.pallas_call(
        _mlstm_kernel,
        grid=(b, s // ts),
        in_specs=[col(COL_DQ), col(COL_DK), col(COL_DV), col(COL_DO),
                  pl.BlockSpec((1, ts, GATE_LANES), lambda bi, i: (bi, i, 0)),
                  vec, vec, pl.BlockSpec((1, w), lambda bi, i: (0, 0))],
        out_specs=pl.BlockSpec((1, ts, w), lambda bi, i: (bi, i, 0)),
        out_shape=jax.ShapeDtypeStruct((b, s, w), BF16),
        scratch_shapes=[pltpu.VMEM((units, hd, 2 * hd), F32), pltpu.VMEM((units, 1, 2 * hd), F32),
                        pltpu.VMEM((units, CHUNK, 2 * hd), F32), big(),
                        pltpu.VMEM((N_HEADS, hd, 2 * hd), F32),
                        pltpu.VMEM((N_HEADS, 1, hd), F32)],
        compiler_params=_cparams(("arbitrary", "arbitrary")),
        name="mlstm",
    )(proj, proj, proj, proj, gates, i_bias_v, f_bias_v, gn_d)


SLAB = 16
SLAB_LANES = 128
MOE_TM = 256
EXPERT_ROW_GROUP = 64
META_ROWS = 8


def _route(h, wr, rb_col):
    tm = h.shape[0]
    logits = lax.dot_general(wr, h, (((1,), (1,)), ((), ())),
                             precision=HIGHEST, preferred_element_type=F32)
    scores = _sigmoid(logits)
    biased = scores + rb_col
    sr = [scores[e:e + 1, :] for e in range(N_EXPERTS)]
    br = [biased[e:e + 1, :] for e in range(N_EXPERTS)]
    n_groups = N_EXPERTS // EXPERTS_PER_GROUP
    group_scores = []
    for g in range(n_groups):
        a, b, cc, d = br[4 * g:4 * g + 4]
        p, q = jnp.maximum(a, b), jnp.minimum(a, b)
        r, s = jnp.maximum(cc, d), jnp.minimum(cc, d)
        group_scores.append(jnp.maximum(p, r) + jnp.maximum(jnp.minimum(p, r), jnp.maximum(q, s)))
    sel = jnp.zeros((1, tm), jnp.int32)
    best = group_scores[0]
    for g in range(1, n_groups):
        better = group_scores[g] > best
        sel = jnp.where(better, g, sel)
        best = jnp.where(better, group_scores[g], best)
    masked = [jnp.where(sel == (e // EXPERTS_PER_GROUP), br[e], -jnp.inf) for e in range(N_EXPERTS)]

    def top1(vals):
        idx = jnp.zeros((1, tm), jnp.int32)
        bv = vals[0]
        for e in range(1, N_EXPERTS):
            better = vals[e] > bv
            idx = jnp.where(better, e, idx)
            bv = jnp.where(better, vals[e], bv)
        return idx

    i1 = top1(masked)
    i2 = top1([jnp.where(i1 == e, -jnp.inf, masked[e]) for e in range(N_EXPERTS)])
    s1 = sum(jnp.where(i1 == e, sr[e], 0.0) for e in range(N_EXPERTS))
    s2 = sum(jnp.where(i2 == e, sr[e], 0.0) for e in range(N_EXPERTS))
    tot = s1 + s2
    return i1, i2, s1 / tot, s2 / tot


def _router_kernel(x_ref, sc_ref, sh_ref, nw_ref, wr_ref, rb_ref, hs_ref, meta_ref, wts_ref, cnt_ref,
                   run):
    @pl.when((pl.program_id(0) == 0) & (pl.program_id(1) == 0))
    def _():
        run[...] = jnp.zeros(run.shape, F32)

    h = _modulated_norm(x_ref[0], nw_ref[...], sc_ref[0], sh_ref[0])
    tm = h.shape[0]
    for s in range(SLAB):
        hs_ref[pl.ds(s, tm, stride=SLAB), :] = h[:, s * SLAB_LANES:(s + 1) * SLAB_LANES]
    i1, i2, w1, w2 = _route(h, wr_ref[...], rb_ref[...][:, 0:1])
    oh1 = jnp.concatenate([(i1 == e).astype(F32) for e in range(N_EXPERTS)], axis=0)
    oh2 = jnp.concatenate([(i2 == e).astype(F32) for e in range(N_EXPERTS)], axis=0)
    before = (lax.broadcasted_iota(jnp.int32, (tm, tm), 0)
              < lax.broadcasted_iota(jnp.int32, (tm, tm), 1)).astype(BF16)
    excl1 = jnp.dot(oh1.astype(BF16), before, preferred_element_type=F32)
    excl2 = jnp.dot(oh2.astype(BF16), before, preferred_element_type=F32)
    tot1 = jnp.sum(oh1, axis=1, keepdims=True)
    tot2 = jnp.sum(oh2, axis=1, keepdims=True)
    base = run[...][:, 0:1]
    rank1 = jnp.sum(oh1 * (base + excl1), axis=0, keepdims=True)
    rank2 = jnp.sum(oh2 * (base + tot1 + excl2), axis=0, keepdims=True)
    run[...] = run[...] + (tot1 + tot2)
    cnt_ref[...] = run[...]
    meta_ref[...] = jnp.concatenate(
        [i1, i2, rank1.astype(jnp.int32), rank2.astype(jnp.int32),
         jnp.zeros((META_ROWS - 4, tm), jnp.int32)], axis=0)
    wts_ref[...] = jnp.concatenate([w1, w2, jnp.zeros((META_ROWS - 2, tm), F32)], axis=0)


def router(x, scale, shift, norm_w, w_router_t, router_bias_col, tm=512):
    b, s, d = x.shape
    nt = s // tm
    t = b * s
    return pl.pallas_call(
        _router_kernel,
        grid=(b, nt),
        in_specs=[pl.BlockSpec((1, tm, d), lambda bi, i: (bi, i, 0)),
                  pl.BlockSpec((1, 1, d), lambda bi, i: (bi, 0, 0)),
                  pl.BlockSpec((1, 1, d), lambda bi, i: (bi, 0, 0)),
                  pl.BlockSpec((1, d), lambda bi, i: (0, 0)),
                  pl.BlockSpec((N_EXPERTS, d), lambda bi, i: (0, 0)),
                  pl.BlockSpec((N_EXPERTS, GATE_LANES), lambda bi, i: (0, 0))],
        out_specs=[pl.BlockSpec((tm * SLAB, SLAB_LANES), lambda bi, i: (bi * nt + i, 0)),
                   pl.BlockSpec((META_ROWS, tm), lambda bi, i: (0, bi * nt + i)),
                   pl.BlockSpec((META_ROWS, tm), lambda bi, i: (0, bi * nt + i)),
                   pl.BlockSpec((N_EXPERTS, GATE_LANES), lambda bi, i: (0, 0))],
        out_shape=[jax.ShapeDtypeStruct((t * SLAB, SLAB_LANES), F32),
                   jax.ShapeDtypeStruct((META_ROWS, t), jnp.int32),
                   jax.ShapeDtypeStruct((META_ROWS, t), F32),
                   jax.ShapeDtypeStruct((N_EXPERTS, GATE_LANES), F32)],
        scratch_shapes=[pltpu.VMEM((N_EXPERTS, GATE_LANES), F32)],
        compiler_params=_cparams(("arbitrary", "arbitrary")),
        name="router",
    )(x, scale, shift, norm_w, w_router_t, router_bias_col)


def routing_tables(meta, counts, tm=MOE_TM):
    t = meta.shape[1]
    cnt = counts[:, 0].astype(jnp.int32)
    padded = (cnt + tm - 1) // tm * tm
    ends = jnp.cumsum(padded)
    off = ends - padded
    eid = meta[0:2]
    pair_off = sum(jnp.where(eid == e, off[e], 0) for e in range(N_EXPERTS))
    pos = pair_off + meta[2:4]
    n_tiles = 2 * t // tm + N_EXPERTS
    tile_start = jnp.arange(n_tiles, dtype=jnp.int32) * tm
    tile_expert = jnp.minimum(jnp.sum((ends[None, :] <= tile_start[:, None]).astype(jnp.int32), axis=1),
                              N_EXPERTS - 1).astype(jnp.int32)
    n_valid = (ends[-1] // tm).astype(jnp.int32).reshape(1)
    return pos.reshape(-1).astype(jnp.int32), tile_expert, n_valid


def _slab_rows(i):
    return pl.ds(pl.multiple_of(i * SLAB, SLAB), SLAB)


def _dispatch_kernel(pos_ref, hs_ref, xz_ref, xs_ref, pair_ref, sem):
    del xz_ref
    tmd = hs_ref.shape[0] // SLAB
    t = pos_ref.shape[0] // 2
    base = pl.program_id(0) * tmd

    @pl.when(pl.program_id(0) == 0)
    def _():
        def fill(r, carry):
            pair_ref[r] = -1
            return carry

        lax.fori_loop(0, pair_ref.shape[0], fill, 0, unroll=64)

    def issue(r, carry):
        for k in range(2):
            p = pos_ref[k * t + base + r]
            pair_ref[p] = k * t + base + r
            pltpu.make_async_copy(hs_ref.at[_slab_rows(r), :], xs_ref.at[_slab_rows(p), :],
                                  sem).start(priority=k)
        return carry

    lax.fori_loop(0, tmd, issue, 0, unroll=8)
    for k in range(2):
        pltpu.make_async_copy(hs_ref, xs_ref.at[pl.ds(0, tmd * SLAB), :], sem).wait()


def dispatch(pos, hs, xs_init, tmd=512):
    t = hs.shape[0] // SLAB
    return pl.pallas_call(
        _dispatch_kernel,
        grid_spec=pltpu.PrefetchScalarGridSpec(
            num_scalar_prefetch=1,
            grid=(t // tmd,),
            in_specs=[pl.BlockSpec((tmd * SLAB, SLAB_LANES), lambda i, pos: (i, 0)),
                      pl.BlockSpec(memory_space=pl.ANY)],
            out_specs=[pl.BlockSpec(memory_space=pl.ANY), pl.BlockSpec(memory_space=pltpu.SMEM)],
            scratch_shapes=[pltpu.SemaphoreType.DMA]),
        out_shape=[jax.ShapeDtypeStruct(xs_init.shape, F32),
                   jax.ShapeDtypeStruct((xs_init.shape[0] // SLAB,), jnp.int32)],
        input_output_aliases={2: 0},
        compiler_params=_cparams(("arbitrary",)),
        name="moe_dispatch",
    )(pos, hs, xs_init)


def _experts_kernel(te_ref, nv_ref, pair_ref, xs_ref, wg_ref, wu_ref, wd_ref, yi_ref, yt_ref,
                    wgb, wub, wdb, yblk, sem):
    del yi_ref
    j = pl.program_id(0)
    nv = nv_ref[0]
    tm = xs_ref.shape[0] // SLAB

    def wait_rows():
        pltpu.make_async_copy(yblk, yt_ref.at[pl.ds(0, tm * SLAB), :], sem).wait()

    @pl.when(j < nv)
    def _():
        prev = te_ref[jnp.maximum(j - 1, 0)]

        @pl.when((j == 0) | (prev != te_ref[j]))
        def _():
            wgb[...] = wg_ref[0, 0].astype(BF16)
            wub[...] = wu_ref[0, 0].astype(BF16)
            wdb[...] = wd_ref[0, 0].astype(BF16)

        x = jnp.concatenate([xs_ref[pl.ds(s, tm, stride=SLAB), :] for s in range(SLAB)],
                            axis=1).astype(BF16)
        hg = jnp.dot(x, wgb[...], preferred_element_type=F32)
        hu = jnp.dot(x, wub[...], preferred_element_type=F32)
        y = _dot(_silu(hg) * hu, wdb[...])

        @pl.when(j > 0)
        def _():
            wait_rows()

        spare = yt_ref.shape[0] // SLAB - N_EXPERTS * tm + te_ref[j] * tm
        for g0 in range(0, tm, EXPERT_ROW_GROUP):
            for s in range(SLAB):
                yblk[pl.ds(g0 * SLAB + s, EXPERT_ROW_GROUP, stride=SLAB), :] = (
                    y[g0:g0 + EXPERT_ROW_GROUP, s * SLAB_LANES:(s + 1) * SLAB_LANES])
            for r in range(g0, g0 + EXPERT_ROW_GROUP):
                pair = pair_ref[j * tm + r]
                d = jnp.where(pair < 0, spare + r, pair)
                pltpu.make_async_copy(yblk.at[pl.ds(r * SLAB, SLAB), :], yt_ref.at[_slab_rows(d), :],
                                      sem).start(priority=r % 2)

        @pl.when(j == nv - 1)
        def _():
            wait_rows()


def grouped_experts(tile_expert, n_valid, row_pair, xs, w_gate, w_up, w_down, layer, yt_init,
                    tm=MOE_TM):
    _, ne, d, dff = w_gate.shape
    n_tiles = tile_expert.shape[0]

    def row_tile(j, te, nv, pair):
        return (jnp.minimum(j, nv[0] - 1), 0)

    def expert(j, te, nv, pair):
        return (layer, te[jnp.minimum(j, nv[0] - 1)], 0, 0)

    return pl.pallas_call(
        _experts_kernel,
        grid_spec=pltpu.PrefetchScalarGridSpec(
            num_scalar_prefetch=3,
            grid=(n_tiles,),
            in_specs=[pl.BlockSpec((tm * SLAB, SLAB_LANES), row_tile),
                      pl.BlockSpec((1, 1, d, dff), expert),
                      pl.BlockSpec((1, 1, d, dff), expert),
                      pl.BlockSpec((1, 1, dff, d), expert),
                      pl.BlockSpec(memory_space=pl.ANY)],
            out_specs=pl.BlockSpec(memory_space=pl.ANY),
            scratch_shapes=[pltpu.VMEM((d, dff), BF16), pltpu.VMEM((d, dff), BF16),
                            pltpu.VMEM((dff, d), BF16), pltpu.VMEM((tm * SLAB, SLAB_LANES), F32),
                            pltpu.SemaphoreType.DMA]),
        out_shape=jax.ShapeDtypeStruct(yt_init.shape, F32),
        input_output_aliases={7: 0},
        compiler_params=_cparams(("arbitrary",)),
        name="moe_experts",
    )(tile_expert, n_valid, row_pair, xs, w_gate, w_up, w_down, yt_init)


def _combine_kernel(x_ref, y1_ref, y2_ref, w_ref, gate_ref, nw_ref, o_ref, *, final_norm):
    tm = x_ref.shape[1]
    w = w_ref[...]
    w1 = w[:, 0:1]
    w2 = w[:, 1:2]
    parts = [w1 * y1_ref[pl.ds(s, tm, stride=SLAB), :] + w2 * y2_ref[pl.ds(s, tm, stride=SLAB), :]
             for s in range(SLAB)]
    out = x_ref[0] + gate_ref[0] * jnp.concatenate(parts, axis=1)
    if final_norm:
        ms = jnp.mean(out * out, axis=-1, keepdims=True)
        out = out * lax.rsqrt(ms + EPS) * nw_ref[...]
    o_ref[0] = out


def combine(x, yt, w_cols, gate, norm_w, final_norm, tm=256):
    b, s, d = x.shape
    nt = s // tm
    t = b * s
    return pl.pallas_call(
        functools.partial(_combine_kernel, final_norm=final_norm),
        grid=(b, nt),
        in_specs=[pl.BlockSpec((1, tm, d), lambda bi, i: (bi, i, 0)),
                  pl.BlockSpec((tm * SLAB, SLAB_LANES), lambda bi, i: (bi * nt + i, 0)),
                  pl.BlockSpec((tm * SLAB, SLAB_LANES), lambda bi, i: (t // tm + bi * nt + i, 0)),
                  pl.BlockSpec((tm, META_ROWS), lambda bi, i: (bi * nt + i, 0)),
                  pl.BlockSpec((1, 1, d), lambda bi, i: (bi, 0, 0)),
                  pl.BlockSpec((1, d), lambda bi, i: (0, 0))],
        out_specs=pl.BlockSpec((1, tm, d), lambda bi, i: (bi, i, 0)),
        out_shape=jax.ShapeDtypeStruct((b, s, d), F32),
        compiler_params=_cparams(("arbitrary", "arbitrary")),
        name="moe_combine",
    )(x, yt, yt, w_cols, gate, norm_w)


def _split_w_in(w_in):
    gw = GROUP_WIDTH
    a_end = 3 * gw
    b_end = a_end + 2 * gw
    c_main_end = b_end + 4 * gw
    c_end = c_main_end + 2 * N_HEADS
    d_main_end = c_end + 4 * gw
    main = jnp.concatenate([w_in[:, :c_main_end], w_in[:, c_end:d_main_end]], axis=1)
    small = jnp.concatenate([w_in[:, c_main_end:c_end], w_in[:, d_main_end:]], axis=1)
    small = jnp.pad(small, ((0, 0), (0, GATE_LANES - small.shape[1])))
    return main.astype(BF16), small.astype(BF16)


def _lanes(vec, offset):
    return jnp.zeros((1, GATE_LANES), F32).at[0, offset:offset + N_HEADS].set(vec)


def _rope_tables(s):
    half = HEAD_DIM // 2
    inv_freq = ROPE_THETA ** (-jnp.arange(half, dtype=F32) / half)
    ang = jnp.arange(s, dtype=F32)[:, None] * inv_freq[None, :]
    cos, sin = jnp.cos(ang), jnp.sin(ang)
    return jnp.concatenate([cos, cos], axis=1), jnp.concatenate([-sin, sin], axis=1)


def hybrid_mixer_layer(x, sc1, sh1, g1, norm_w, w_in, w_out, gn_a, conv_b_w, conv_b_b, w_rg, b_rg,
                       w_ig, b_ig, lru_lambda, gn_b, conv_c_w, gdn_a_log, gdn_dt_bias, gdn_norm,
                       mlstm_i_bias, mlstm_f_bias, gn_d, rope):
    w_main, w_small = _split_w_in(w_in)
    proj, gates = in_projection(x, sc1, sh1, norm_w.reshape(1, -1), w_main, w_small)
    row = lambda v: v.reshape(1, -1)
    y_a = dilated_attention_mixer(proj, rope[0], rope[1], row(gn_a))
    y_b = rg_lru_mixer(proj, conv_b_w, row(conv_b_b), w_rg.astype(BF16), row(b_rg),
                       w_ig.astype(BF16), row(b_ig), row(lru_lambda), row(gn_b))
    y_c = gated_delta_mixer(proj, gates, conv_c_w, _lanes(gdn_a_log, N_HEADS),
                            _lanes(gdn_dt_bias, N_HEADS), row(jnp.tile(gdn_norm, N_HEADS)))
    y_d = mlstm_mixer(proj, gates, _lanes(mlstm_i_bias, 2 * N_HEADS),
                      _lanes(mlstm_f_bias, 3 * N_HEADS), row(gn_d))
    return out_projection(x, (y_a, y_b, y_c, y_d), w_out.astype(BF16), g1)


def moe_layer(x, sc2, sh2, g2, norm_w, w_router, router_bias, w_gate, w_up, w_down, layer, xs_buf,
              yt_buf, final_norm_w, apply_final_norm):
    rb = jnp.broadcast_to(router_bias.reshape(-1, 1), (N_EXPERTS, GATE_LANES))
    hs, meta, wts, counts = router(x, sc2, sh2, norm_w.reshape(1, -1), w_router.T, rb)
    pos, tile_expert, n_valid = routing_tables(meta, counts)
    xs, row_pair = dispatch(pos, hs, xs_buf)
    yt = grouped_experts(tile_expert, n_valid, row_pair, xs, w_gate, w_up, w_down, layer, yt_buf)
    return combine(x, yt, wts.T, g2, final_norm_w, apply_final_norm), xs, yt


def moe_buffers(n_tokens):
    rows = 2 * n_tokens + N_EXPERTS * MOE_TM
    return jnp.zeros((rows * SLAB, SLAB_LANES), F32), jnp.zeros((rows * SLAB, SLAB_LANES), F32)


def kernel(x, c, norm_mix, norm_ffn, norm_final, w_ada, b_ada, w_in, w_out, gn_a, conv_b_w, conv_b_b, w_rg, b_rg, w_ig, b_ig, lru_lambda, gn_b, conv_c_w, gdn_a_log, gdn_dt_bias, gdn_norm, mlstm_i_bias, mlstm_f_bias, gn_d, w_router, router_bias, moe_w_gate, moe_w_up, moe_w_down):
    depth = w_ada.shape[0]
    b, s, d = x.shape
    mod = ada_modulation(c, w_ada, b_ada)
    rope = _rope_tables(s)
    xs_buf, yt_buf = moe_buffers(b * s)
    for l in range(depth):
        sh1, sc1, g1, sh2, sc2, g2 = [mod[l, :, None, i * d:(i + 1) * d] for i in range(6)]
        x = hybrid_mixer_layer(x, sc1, sh1, g1, norm_mix[l], w_in[l], w_out[l], gn_a[l],
                               conv_b_w[l], conv_b_b[l], w_rg[l], b_rg[l], w_ig[l], b_ig[l],
                               lru_lambda[l], gn_b[l], conv_c_w[l], gdn_a_log[l], gdn_dt_bias[l],
                               gdn_norm[l], mlstm_i_bias[l], mlstm_f_bias[l], gn_d[l], rope)
        x, xs_buf, yt_buf = moe_layer(x, sc2, sh2, g2, norm_ffn[l], w_router, router_bias,
                                      moe_w_gate, moe_w_up, moe_w_down, l, xs_buf, yt_buf,
                                      norm_final.reshape(1, -1), l == depth - 1)
    return x
```
